```python
import jax, jax.numpy as jnp
from jax import lax
import numpy as np

D_MODEL = 1024
BATCH = 2
SEQ = 16384
DEPTH = 2
DEC_BATCH = 16
DEC_SEQ = 32
PAST_LEN = 2048

CHUNK = 64
Q_BLOCK = 128
EPS = 1e-6
MLA_HEADS = 8
QK_NOPE = 64
QK_ROPE = 32
V_HEAD = 64
Q_LORA = 256
KV_LORA = 256
ROPE_THETA = 10000.0
QK_DIM = QK_NOPE + QK_ROPE
ATTN_SCALE = QK_DIM ** -0.5
GLA_HEADS = 4
GLA_DK = 64
GLA_DV = 128
GATE_LR = 16
GATE_TAU = 16.0
MIX_WIDTH = MLA_HEADS * V_HEAD + GLA_HEADS * GLA_DV
IN_SIZES = (Q_LORA, KV_LORA, QK_ROPE, GLA_HEADS * GLA_DK, GLA_HEADS * GLA_DK,
            GLA_HEADS * GLA_DV, GLA_HEADS * GLA_DV, GATE_LR)
IN_WIDTH = int(sum(IN_SIZES))
IN_SPLITS = tuple(int(s) for s in np.cumsum(IN_SIZES)[:-1])
D_FF = -(-8 * D_MODEL // (3 * 256)) * 256

kernel_name = "hymba_mla_gla_streaming_step"


def rmsnorm(x, g):
    xf = x.astype(jnp.float32)
    y = xf * lax.rsqrt(jnp.mean(xf * xf, axis=-1, keepdims=True) + EPS)
    return (y * g.astype(jnp.float32)).astype(x.dtype)


def rope(x, pos):
    half = QK_ROPE // 2
    inv = ROPE_THETA ** (-jnp.arange(half, dtype=jnp.float32) / half)
    ang = pos.astype(jnp.float32)[:, None] * inv[None, :]
    ang = ang.reshape((ang.shape[0],) + (1,) * (x.ndim - 3) + (half,))
    cos, sin = jnp.cos(ang), jnp.sin(ang)
    x1 = x[..., :half].astype(jnp.float32)
    x2 = x[..., half:].astype(jnp.float32)
    return jnp.concatenate([x1 * cos - x2 * sin, x1 * sin + x2 * cos], axis=-1).astype(x.dtype)


def mixer_proj(h, pos, w_in, g_qn, w_uq, g_kvn, w_a2, b_a2):
    B, T, _ = h.shape
    z = h @ w_in
    c_q, c_kv, k_r, gq, gk, gv, gg, ga = jnp.split(z, IN_SPLITS, axis=-1)
    q = (rmsnorm(c_q, g_qn) @ w_uq).reshape(B, T, MLA_HEADS, QK_DIM)
    q = jnp.concatenate([q[..., :QK_NOPE], rope(q[..., QK_NOPE:], pos)], axis=-1)
    c_kv = rmsnorm(c_kv, g_kvn)
    k_r = rope(k_r, pos)
    gq = gq.reshape(B, T, GLA_HEADS, GLA_DK) * (GLA_DK ** -0.5)
    gk = gk.reshape(B, T, GLA_HEADS, GLA_DK)
    gv = gv.reshape(B, T, GLA_HEADS, GLA_DV)
    gg = gg.reshape(B, T, GLA_HEADS, GLA_DV)
    lg = jax.nn.log_sigmoid((ga @ w_a2 + b_a2).astype(jnp.float32)) / GATE_TAU
    lg = lg.reshape(B, T, GLA_HEADS, GLA_DK)
    return q, c_kv, k_r, gq, gk, gv, gg, lg


def mla_kv(c_kv, k_r, w_ukv):
    B, T, _ = c_kv.shape
    kv = (c_kv @ w_ukv).reshape(B, T, MLA_HEADS, QK_NOPE + V_HEAD)
    k_nope, v = kv[..., :QK_NOPE], kv[..., QK_NOPE:]
    k = jnp.concatenate([k_nope, jnp.broadcast_to(k_r[:, :, None, :], (B, T, MLA_HEADS, QK_ROPE))], axis=-1)
    return k, v


def chunk_causal_attn(q, k, v, q_pos, k_pos):
    s = jnp.einsum('bqhd,bkhd->bhqk', q, k, preferred_element_type=jnp.float32) * ATTN_SCALE
    allowed = (k_pos[None, :] // CHUNK) <= (q_pos[:, None] // CHUNK)
    s = jnp.where(allowed[None, None], s, -jnp.inf)
    p = jax.nn.softmax(s, axis=-1).astype(v.dtype)
    return jnp.einsum('bhqk,bkhv->bqhv', p, v)


def mla_prompt(q, k, v, pos):
    B, T, H, D = q.shape
    nb = T // Q_BLOCK
    qb = q.reshape(B, nb, Q_BLOCK, H, D).swapaxes(0, 1)
    pb = pos.reshape(nb, Q_BLOCK)
    o = lax.map(lambda a: chunk_causal_attn(a[0], k, v, a[1], pos), (qb, pb))
    return o.swapaxes(0, 1).reshape(B, T, H, V_HEAD)


def gla_chunk(S, q, k, v, lg):
    q = q.astype(jnp.float32)
    k = k.astype(jnp.float32)
    v = v.astype(jnp.float32)
    L = q.shape[1]
    b = jnp.cumsum(lg, axis=1)
    o_inter = jnp.einsum('blhk,bhkv->blhv', q * jnp.exp(b), S)
    mask = jnp.tril(jnp.ones((L, L), dtype=bool))
    diff = b[:, :, None] - b[:, None, :]
    decay = jnp.exp(jnp.where(mask[None, :, :, None, None], diff, -jnp.inf))
    a = jnp.einsum('bihk,bjhk,bijhk->bhij', q, k, decay)
    o_intra = jnp.einsum('bhij,bjhv->bihv', a, v)
    b_last = b[:, -1]
    S_new = jnp.exp(b_last)[..., None] * S + jnp.einsum(
        'bjhk,bjhv->bhkv', k * jnp.exp(b_last[:, None] - b), v)
    return S_new, o_inter + o_intra


def gla_prompt(q, k, v, lg):
    B, T, H, DK = q.shape
    nc = T // CHUNK

    def to_chunks(a):
        return a.reshape((B, nc, CHUNK) + a.shape[2:]).swapaxes(0, 1)

    S0 = jnp.zeros((B, H, DK, GLA_DV), jnp.float32)
    S_fin, o = lax.scan(lambda S, xs: gla_chunk(S, *xs), S0,
                        (to_chunks(q), to_chunks(k), to_chunks(v), to_chunks(lg)))
    return S_fin, o.swapaxes(0, 1).reshape(B, T, H, GLA_DV)


def mixer_out(o_mla, o_gla, gate, g_on, w_o):
    B, T = o_mla.shape[:2]
    o_g = (rmsnorm(o_gla, g_on) * jax.nn.silu(gate.astype(jnp.float32))).astype(o_mla.dtype)
    cat = jnp.concatenate([o_mla.reshape(B, T, MLA_HEADS * V_HEAD),
                           o_g.reshape(B, T, GLA_HEADS * GLA_DV)], axis=-1)
    return cat @ w_o


def swiglu(h, w_gu, w_down):
    a, u = jnp.split(h @ w_gu, 2, axis=-1)
    return (jax.nn.silu(a) * u) @ w_down


def setup_inputs(seed: int = 0) -> dict:
    key = jax.random.key(seed)
    ks = jax.random.split(key, 20)

    def nrm(k, shape, scale):
        return jax.random.normal(k, shape, jnp.float32) * scale

    def gain(k, shape):
        return 1.0 + 0.05 * jax.random.normal(k, shape, jnp.float32)

    return {
        "x_prompt": nrm(ks[0], (BATCH, SEQ, D_MODEL), 1.0),
        "x_sample": nrm(ks[1], (DEC_BATCH, DEC_SEQ, D_MODEL), 1.0),
        "cache_ckv": nrm(ks[2], (DEPTH, DEC_BATCH, PAST_LEN, KV_LORA), 1.0),
        "cache_krope": nrm(ks[3], (DEPTH, DEC_BATCH, PAST_LEN, QK_ROPE), 1.0),
        "state_gla": nrm(ks[4], (DEPTH, DEC_BATCH, GLA_HEADS, GLA_DK, GLA_DV), 0.5),
        "g_attn": gain(ks[5], (DEPTH, D_MODEL)),
        "w_in": nrm(ks[6], (DEPTH, D_MODEL, IN_WIDTH), D_MODEL ** -0.5),
        "g_qn": gain(ks[7], (DEPTH, Q_LORA)),
        "w_uq": nrm(ks[8], (DEPTH, Q_LORA, MLA_HEADS * QK_DIM), Q_LORA ** -0.5),
        "g_kvn": gain(ks[9], (DEPTH, KV_LORA)),
        "w_ukv": nrm(ks[10], (DEPTH, KV_LORA, MLA_HEADS * (QK_NOPE + V_HEAD)), KV_LORA ** -0.5),
        "w_a2": nrm(ks[11], (DEPTH, GATE_LR, GLA_HEADS * GLA_DK), GATE_LR ** -0.5),
        "b_a2": nrm(ks[12], (DEPTH, GLA_HEADS * GLA_DK), 0.1),
        "g_gla_on": gain(ks[13], (DEPTH, GLA_DV)),
        "w_o": nrm(ks[14], (DEPTH, MIX_WIDTH, D_MODEL), MIX_WIDTH ** -0.5),
        "g_ffn": gain(ks[15], (DEPTH, D_MODEL)),
        "w_gu": nrm(ks[16], (DEPTH, D_MODEL, 2 * D_FF), D_MODEL ** -0.5),
        "w_down": nrm(ks[17], (DEPTH, D_FF, D_MODEL), D_FF ** -0.5),
        "g_final": gain(ks[18], (D_MODEL,)),
    }


def reference(x_prompt, x_sample, cache_ckv, cache_krope, state_gla,
              g_attn, w_in, g_qn, w_uq, g_kvn, w_ukv, w_a2, b_a2, g_gla_on, w_o,
              g_ffn, w_gu, w_down, g_final):
    seq = x_prompt.shape[1]
    dec_seq = x_sample.shape[1]
    past = cache_ckv.shape[2]
    pos_p = jnp.arange(seq, dtype=jnp.int32)
    pos_s = past + jnp.arange(dec_seq, dtype=jnp.int32)
    pos_all = jnp.arange(past + dec_seq, dtype=jnp.int32)

    xp, xs = x_prompt, x_sample
    ckv_p, kr_p, st_p, ckv_s, kr_s, st_s = [], [], [], [], [], []
    for l in range(DEPTH):
        proj = (w_in[l], g_qn[l], w_uq[l], g_kvn[l], w_a2[l], b_a2[l])
        h = rmsnorm(xp, g_attn[l])
        q, c_kv, k_r, gq, gk, gv, gg, lg = mixer_proj(h, pos_p, *proj)
        k, v = mla_kv(c_kv, k_r, w_ukv[l])
        o_mla = mla_prompt(q, k, v, pos_p)
        S_fin, o_gla = gla_prompt(gq, gk, gv, lg)
        xp = xp + mixer_out(o_mla, o_gla, gg, g_gla_on[l], w_o[l])
        xp = xp + swiglu(rmsnorm(xp, g_ffn[l]), w_gu[l], w_down[l])
        ckv_p.append(c_kv)
        kr_p.append(k_r)
        st_p.append(S_fin)
        h = rmsnorm(xs, g_attn[l])
        q, c_kv, k_r, gq, gk, gv, gg, lg = mixer_proj(h, pos_s, *proj)
        ckv_full = jnp.concatenate([cache_ckv[l].astype(c_kv.dtype), c_kv], axis=1)
        kr_full = jnp.concatenate([cache_krope[l].astype(k_r.dtype), k_r], axis=1)
        k, v = mla_kv(ckv_full, kr_full, w_ukv[l])
        o_mla = chunk_causal_attn(q, k, v, pos_s, pos_all)
        S_new, o_gla = gla_chunk(state_gla[l].astype(jnp.float32), gq, gk, gv, lg)
        xs = xs + mixer_out(o_mla, o_gla, gg, g_gla_on[l], w_o[l])
        xs = xs + swiglu(rmsnorm(xs, g_ffn[l]), w_gu[l], w_down[l])
        ckv_s.append(c_kv)
        kr_s.append(k_r)
        st_s.append(S_new)

    y_prompt = rmsnorm(xp, g_final)
    y_sample = rmsnorm(xs, g_final)
    new_ckv_prompt = jnp.stack(ckv_p)
    new_krope_prompt = jnp.stack(kr_p)
    new_gla_prompt = jnp.stack(st_p)
    new_ckv_sample = jnp.stack(ckv_s)
    new_krope_sample = jnp.stack(kr_s)
    new_gla_sample = jnp.stack(st_s)
    return (y_prompt, y_sample, new_ckv_prompt, new_krope_prompt, new_gla_prompt,
            new_ckv_sample, new_krope_sample, new_gla_sample)
```

```python
import functools

import jax
import jax.numpy as jnp
import numpy as np
from jax import lax
from jax.experimental import pallas as pl
from jax.experimental.pallas import tpu as pltpu

F32 = jnp.float32
BF16 = jnp.bfloat16

CHUNK = 64
EPS = 1e-6
MLA_HEADS = 8
QK_NOPE = 64
QK_ROPE = 32
ROPE_HALF = QK_ROPE // 2
V_HEAD = 64
Q_LORA = 256
KV_LORA = 256
ROPE_THETA = 10000.0
ATTN_SCALE = (QK_NOPE + QK_ROPE) ** -0.5
GLA_HEADS = 4
GLA_DK = 64
GLA_DV = 128
GATE_LR = 16
GATE_TAU = 16.0
GLA_K = GLA_HEADS * GLA_DK
GLA_V = GLA_HEADS * GLA_DV

LANES = 128
HEAD_PAD = 128
SUB = 16
NEG = -1e30
VMEM_LIMIT = 48 * 1024 * 1024

Z_CQ = 0
Z_CKV = Z_CQ + Q_LORA
Z_GQ = Z_CKV + KV_LORA
Z_GK = Z_GQ + GLA_K
Z_GV = Z_GK + GLA_K
Z_GG = Z_GV + GLA_V
Z_SM = Z_GG + GLA_V
Z_W = Z_SM + LANES


def _nt(a, b):
    return lax.dot_general(a, b, (((1,), (1,)), ((), ())), preferred_element_type=F32)


def _tn(a, b):
    return lax.dot_general(a, b, (((0,), (0,)), ((), ())), preferred_element_type=F32)


def _dot(a, b):
    return jnp.dot(a, b, preferred_element_type=F32)


def _rms(x, g):
    return x * lax.rsqrt(jnp.mean(x * x, axis=-1, keepdims=True) + EPS) * g


def _proj_kernel(emit_kv, x_ref, cr_ref, sr_ref, ct_ref, st_ref, gat_ref, win_ref, gqn_ref, wqn_ref, wqr_ref,
                 gkv_ref, wk_ref, wvt_ref, wa2_ref, ba2_ref, *outs):
    if emit_kv:
        ckv_ref, kr_ref, qt_ref, k_ref, vt_ref, gq_ref, gk_ref, gv_ref, gg_ref, lg_ref = outs
    else:
        ckv_ref, kr_ref, qt_ref, gq_ref, gk_ref, gv_ref, gg_ref, lg_ref = outs
    x = x_ref[0]
    h = _rms(x, gat_ref[...]).astype(BF16)
    z = _dot(h, win_ref[...])

    cqn = _rms(z[:, Z_CQ:Z_CQ + Q_LORA], gqn_ref[...]).astype(BF16)
    n_nope = MLA_HEADS * QK_NOPE
    n_rope = MLA_HEADS * ROPE_HALF
    qt_ref[0, 0:n_nope, :] = _nt(wqn_ref[...], cqn).astype(BF16)
    qr = _nt(wqr_ref[...], cqn)
    x1, x2 = qr[:n_rope], qr[n_rope:]
    ct, st = ct_ref[...], st_ref[...]
    qt_ref[0, n_nope:n_nope + n_rope, :] = (x1 * ct - x2 * st).astype(BF16)
    qt_ref[0, n_nope + n_rope:, :] = (x1 * st + x2 * ct).astype(BF16)

    ckv = _rms(z[:, Z_CKV:Z_CKV + KV_LORA], gkv_ref[...])
    ckv_ref[0] = ckv
    u = z[:, Z_SM:Z_SM + LANES]
    lane = lax.broadcasted_iota(jnp.int32, u.shape, 1)
    partner = jnp.where(lane < ROPE_HALF, pltpu.roll(u, LANES - ROPE_HALF, 1), pltpu.roll(u, ROPE_HALF, 1))
    kr = u * cr_ref[...] + partner * sr_ref[...]
    kr_ref[0] = kr[:, :QK_ROPE]
    if emit_kv:
        ckv_b = ckv.astype(BF16)
        k_ref[0] = _dot(jnp.concatenate([ckv_b, kr.astype(BF16)], axis=1), wk_ref[...]).astype(BF16)
        vt_ref[0] = _nt(wvt_ref[...], ckv_b).astype(BF16)

    gq_ref[0] = z[:, Z_GQ:Z_GQ + GLA_K]
    gk_ref[0] = z[:, Z_GK:Z_GK + GLA_K]
    gv_ref[0] = z[:, Z_GV:Z_GV + GLA_V].astype(BF16)
    gg_ref[0] = z[:, Z_GG:Z_GG + GLA_V]
    xg = _dot(u.astype(BF16), wa2_ref[...]) + ba2_ref[...]
    lg_ref[0] = (jnp.minimum(xg, 0.0) - jnp.log1p(jnp.exp(-jnp.abs(xg)))) * (1.0 / GATE_TAU)


def _proj(x, tabs, lw, tm, emit_kv):
    B, T, D = x.shape
    nt = T // tm
    cr, sr, ct, st = tabs
    tok = lambda w: pl.BlockSpec((1, tm, w), lambda b, t: (b, t, 0))
    feat = lambda r: pl.BlockSpec((1, r, tm), lambda b, t: (b, 0, t))
    full = lambda a: pl.BlockSpec(a.shape, lambda b, t: (0,) * a.ndim)
    weights = (lw["g_attn"], lw["w_in"], lw["g_qn"], lw["w_qnT"], lw["w_qrT"], lw["g_kvn"], lw["w_k"], lw["w_vT"],
               lw["w_a2"], lw["b_a2"])
    in_specs = [tok(D),
                pl.BlockSpec((tm, LANES), lambda b, t: (t, 0)), pl.BlockSpec((tm, LANES), lambda b, t: (t, 0)),
                pl.BlockSpec((LANES, tm), lambda b, t: (0, t)), pl.BlockSpec((LANES, tm), lambda b, t: (0, t))]
    in_specs += [full(w) for w in weights]
    S = jax.ShapeDtypeStruct
    nq = MLA_HEADS * (QK_NOPE + QK_ROPE)
    out_shape = [S((B, T, KV_LORA), F32), S((B, T, QK_ROPE), F32), S((B, nq, T), BF16)]
    out_specs = [tok(KV_LORA), tok(QK_ROPE), feat(nq)]
    if emit_kv:
        out_shape += [S((B, T, MLA_HEADS * HEAD_PAD), BF16), S((B, MLA_HEADS * V_HEAD, T), BF16)]
        out_specs += [tok(MLA_HEADS * HEAD_PAD), feat(MLA_HEADS * V_HEAD)]
    out_shape += [S((B, T, GLA_K), F32), S((B, T, GLA_K), F32), S((B, T, GLA_V), BF16), S((B, T, GLA_V), F32),
                  S((B, T, GLA_K), F32)]
    out_specs += [tok(GLA_K), tok(GLA_K), tok(GLA_V), tok(GLA_V), tok(GLA_K)]
    return pl.pallas_call(
        functools.partial(_proj_kernel, emit_kv),
        grid=(B, nt), in_specs=in_specs, out_specs=out_specs, out_shape=out_shape,
        compiler_params=pltpu.CompilerParams(dimension_semantics=("parallel", "parallel"),
                                             vmem_limit_bytes=VMEM_LIMIT),
        name="proj_kv" if emit_kv else "proj",
    )(x, cr, sr, ct, st, *weights)


def _kvproj_kernel(cckv_ref, ckr_ref, nckv_ref, nkr_ref, wkc_ref, wkr_ref, wvt_ref, k_ref, vt_ref):
    ckv = jnp.concatenate([cckv_ref[0], nckv_ref[0]], axis=0).astype(BF16)
    kr = jnp.concatenate([ckr_ref[0], nkr_ref[0]], axis=0).astype(BF16)
    k_ref[0] = (_dot(ckv, wkc_ref[...]) + _dot(kr, wkr_ref[...])).astype(BF16)
    vt_ref[0] = _nt(wvt_ref[...], ckv).astype(BF16)


def _kvproj(cache_ckv, cache_kr, new_ckv, new_kr, lw):
    nb, past, _ = cache_ckv.shape
    new = new_ckv.shape[1]
    tot = past + new
    wkc, wkr = lw["w_k"][:KV_LORA], lw["w_k"][KV_LORA:KV_LORA + QK_ROPE]
    per_b = lambda a: pl.BlockSpec((1,) + a.shape[1:], lambda b: (b, 0, 0))
    full = lambda a: pl.BlockSpec(a.shape, lambda b: (0,) * a.ndim)
    S = jax.ShapeDtypeStruct
    return pl.pallas_call(
        _kvproj_kernel, grid=(nb,),
        in_specs=[per_b(cache_ckv), per_b(cache_kr), per_b(new_ckv), per_b(new_kr), full(wkc), full(wkr),
                  full(lw["w_vT"])],
        out_specs=[pl.BlockSpec((1, tot, MLA_HEADS * HEAD_PAD), lambda b: (b, 0, 0)),
                   pl.BlockSpec((1, MLA_HEADS * V_HEAD, tot), lambda b: (b, 0, 0))],
        out_shape=[S((nb, tot, MLA_HEADS * HEAD_PAD), BF16), S((nb, MLA_HEADS * V_HEAD, tot), BF16)],
        compiler_params=pltpu.CompilerParams(dimension_semantics=("parallel",), vmem_limit_bytes=VMEM_LIMIT),
        name="kvproj",
    )(cache_ckv, cache_kr, new_ckv, new_kr, wkc, wkr, lw["w_vT"])


def _attn_kernel(tq, tk, q_off, qi_ref, ki_ref, last_ref, qt_ref, k_ref, vt_ref, o_ref, m_ref, l_ref, acc_ref):
    p_id = pl.program_id(1)
    qi, ki = qi_ref[p_id], ki_ref[p_id]
    n_nope = MLA_HEADS * QK_NOPE
    n_rope = MLA_HEADS * ROPE_HALF

    @pl.when(ki == 0)
    def _():
        m_ref[...] = jnp.full(m_ref.shape, NEG, F32)
        l_ref[...] = jnp.zeros(l_ref.shape, F32)
        acc_ref[...] = jnp.zeros(acc_ref.shape, F32)

    def step(masked):
        if masked:
            kpos = ki * tk + lax.broadcasted_iota(jnp.int32, (tk, tq), 0)
            qpos = q_off + qi * tq + lax.broadcasted_iota(jnp.int32, (tk, tq), 1)
            allowed = (kpos // CHUNK) <= (qpos // CHUNK)
        for h in range(MLA_HEADS):
            qt_h = jnp.concatenate(
                [qt_ref[0, h * QK_NOPE:(h + 1) * QK_NOPE, :],
                 qt_ref[0, n_nope + h * ROPE_HALF:n_nope + (h + 1) * ROPE_HALF, :],
                 qt_ref[0, n_nope + n_rope + h * ROPE_HALF:n_nope + n_rope + (h + 1) * ROPE_HALF, :],
                 jnp.zeros((HEAD_PAD - QK_NOPE - QK_ROPE, tq), BF16)], axis=0)
            s = _dot(k_ref[0, :, h * HEAD_PAD:(h + 1) * HEAD_PAD], qt_h)
            if masked:
                s = jnp.where(allowed, s, NEG)
            m_old = m_ref[h:h + 1, :]
            m_new = jnp.maximum(m_old, jnp.max(s, axis=0, keepdims=True))
            alpha = jnp.exp(m_old - m_new)
            p = jnp.exp(s - m_new)
            l_ref[h:h + 1, :] = alpha * l_ref[h:h + 1, :] + jnp.sum(p, axis=0, keepdims=True)
            m_ref[h:h + 1, :] = m_new
            rows = slice(h * V_HEAD, (h + 1) * V_HEAD)
            acc_ref[rows, :] = alpha * acc_ref[rows, :] + _dot(vt_ref[0, rows, :], p.astype(BF16))

    needs_mask = (ki * tk + tk - 1) // CHUNK > (q_off + qi * tq) // CHUNK
    pl.when(needs_mask)(lambda: step(True))
    pl.when(jnp.logical_not(needs_mask))(lambda: step(False))

    @pl.when(last_ref[p_id] == 1)
    def _():
        inv = 1.0 / l_ref[...]
        parts = [acc_ref[h * V_HEAD:(h + 1) * V_HEAD, :] * inv[h:h + 1, :] for h in range(MLA_HEADS)]
        o_ref[0] = jnp.concatenate(parts, axis=0).T.astype(BF16)


def _attn(qt, k, vt, tq, tk, q_off):
    B, nqf, Tq = qt.shape
    Tk = k.shape[1]
    nq, nk = Tq // tq, Tk // tk
    qi_l, ki_l, last_l = [], [], []
    for qi in range(nq):
        last_chunk = (q_off + qi * tq + tq - 1) // CHUNK
        kmax = min(nk - 1, ((last_chunk + 1) * CHUNK - 1) // tk)
        for ki in range(kmax + 1):
            qi_l.append(qi), ki_l.append(ki), last_l.append(int(ki == kmax))
    tabs = [jnp.asarray(np.asarray(a, np.int32)) for a in (qi_l, ki_l, last_l)]
    grid_spec = pltpu.PrefetchScalarGridSpec(
        num_scalar_prefetch=3, grid=(B, len(qi_l)),
        in_specs=[pl.BlockSpec((1, nqf, tq), lambda b, p, qi, ki, la: (b, 0, qi[p])),
                  pl.BlockSpec((1, tk, MLA_HEADS * HEAD_PAD), lambda b, p, qi, ki, la: (b, ki[p], 0)),
                  pl.BlockSpec((1, MLA_HEADS * V_HEAD, tk), lambda b, p, qi, ki, la: (b, 0, ki[p]))],
        out_specs=pl.BlockSpec((1, tq, MLA_HEADS * V_HEAD), lambda b, p, qi, ki, la: (b, qi[p], 0)),
        scratch_shapes=[pltpu.VMEM((MLA_HEADS, tq), F32), pltpu.VMEM((MLA_HEADS, tq), F32),
                        pltpu.VMEM((MLA_HEADS * V_HEAD, tq), F32)])
    return pl.pallas_call(
        functools.partial(_attn_kernel, tq, tk, q_off), grid_spec=grid_spec,
        out_shape=jax.ShapeDtypeStruct((B, Tq, MLA_HEADS * V_HEAD), BF16),
        compiler_params=pltpu.CompilerParams(dimension_semantics=("parallel", "arbitrary"),
                                             vmem_limit_bytes=VMEM_LIMIT),
        name="attn",
    )(*tabs, qt, k, vt)


def _gla_kernel(lc, n_chunks, gq_ref, gk_ref, lg_ref, gv_ref, gg_ref, gon_ref, s0_ref, og_ref, sfin_ref, st_ref):
    n_sub = lc // SUB
    n_stack = SUB * n_sub * (n_sub - 1) // 2
    stack_pad = max(LANES, -(-n_stack // LANES) * LANES)

    @pl.when(pl.program_id(1) == 0)
    def _():
        st_ref[...] = s0_ref[0]

    lane_k = lax.broadcasted_iota(jnp.int32, (lc, GLA_K), 1) // GLA_DK
    ri = lax.broadcasted_iota(jnp.int32, (lc, lc), 0)
    ci = lax.broadcasted_iota(jnp.int32, (lc, lc), 1)
    tri = (ci <= ri).astype(BF16)
    e_r = lax.broadcasted_iota(jnp.int32, (GLA_K, GLA_V), 0) // GLA_DK
    e_c = lax.broadcasted_iota(jnp.int32, (GLA_K, GLA_V), 1) // GLA_DV
    e_mat = (e_r == e_c).astype(BF16)
    sub_row = lax.broadcasted_iota(jnp.int32, (SUB, GLA_K), 0)
    st_lane = lax.broadcasted_iota(jnp.int32, (GLA_DV, GLA_K), 1) // GLA_DK
    a_row = lax.broadcasted_iota(jnp.int32, (lc, stack_pad), 0) // SUB
    a_col = lax.broadcasted_iota(jnp.int32, (lc, stack_pad), 1)
    off_mask = jnp.zeros((lc, stack_pad), jnp.bool_)
    seg = 0
    for i_sub in range(1, n_sub):
        off_mask = off_mask | ((a_row == i_sub) & (a_col >= seg) & (a_col < seg + SUB * i_sub))
        seg += SUB * i_sub

    def heads_on_rows(a):
        return jnp.concatenate([jnp.where(lane_k == h, a, 0.0) for h in range(GLA_HEADS)], axis=0).astype(BF16)

    def chunk(c, carry):
        r = pl.ds(pl.multiple_of(c * lc, lc), lc)
        q, k, lg = gq_ref[0, r, :], gk_ref[0, r, :], lg_ref[0, r, :]
        v_b = gv_ref[0, r, :]
        v_f = v_b.astype(F32)
        hi = lg.astype(BF16)
        r1 = lg - hi.astype(F32)
        mid = r1.astype(BF16)
        lo = (r1 - mid.astype(F32)).astype(BF16)
        b = _dot(tri, hi) + _dot(tri, mid) + _dot(tri, lo)
        b_last = b[lc - 1:lc, :]
        st = st_ref[...]

        o_heads = _nt(heads_on_rows(q * jnp.exp(b)), st.astype(BF16))

        kd = (k * jnp.exp(b_last - b)).astype(BF16)
        upd = jnp.zeros((GLA_DV, GLA_K), F32)
        for h in range(GLA_HEADS):
            upd = upd + jnp.where(st_lane == h, _tn(v_b[:, h * GLA_DV:(h + 1) * GLA_DV], kd), 0.0)
        st_ref[...] = jnp.exp(b_last) * st + upd

        o_off = [None] * GLA_HEADS
        if n_sub > 1:
            refs = [jnp.zeros((SUB, GLA_K), F32)]
            kst, vst = [], []
            for i_sub in range(1, n_sub):
                r_i = b[SUB * i_sub - 1:SUB * i_sub, :]
                refs.append(jnp.broadcast_to(r_i, (SUB, GLA_K)))
                kst.append(k[:SUB * i_sub] * jnp.exp(r_i - b[:SUB * i_sub]))
                vst.append(v_b[:SUB * i_sub])
            if stack_pad > n_stack:
                kst.append(jnp.zeros((stack_pad - n_stack, GLA_K), F32))
                vst.append(jnp.zeros((stack_pad - n_stack, GLA_V), BF16))
            kst = jnp.concatenate(kst, axis=0).astype(BF16)
            vst = jnp.concatenate(vst, axis=0)
            qp = q * jnp.exp(b - jnp.concatenate(refs, axis=0))
            a4 = _nt(heads_on_rows(qp), kst)
            for h in range(GLA_HEADS):
                a_h = jnp.where(off_mask, a4[h * lc:(h + 1) * lc], 0.0).astype(BF16)
                o_off[h] = _dot(a_h, vst[:, h * GLA_DV:(h + 1) * GLA_DV])

        o = jnp.concatenate(
            [o_heads[h * lc:(h + 1) * lc] + (o_off[h] if n_sub > 1 else 0.0) for h in range(GLA_HEADS)], axis=1)

        o_diag = []
        for i_sub in range(n_sub):
            rs = slice(SUB * i_sub, SUB * (i_sub + 1))
            q_i, k_i, b_i, v_i = q[rs], k[rs], b[rs], v_f[rs]
            terms = []
            for j in range(SUB):
                arg = jnp.where(sub_row >= j, b_i - b_i[j:j + 1], NEG)
                terms.append(q_i * (k_i[j:j + 1] * jnp.exp(arg)))
            w = _dot(jnp.concatenate(terms, axis=0).astype(BF16), e_mat)
            acc = w[0:SUB] * v_i[0:1]
            for j in range(1, SUB):
                acc = acc + w[j * SUB:(j + 1) * SUB] * v_i[j:j + 1]
            o_diag.append(acc)
        o = o + jnp.concatenate(o_diag, axis=0)

        gon = gon_ref[...]
        normed = jnp.concatenate(
            [_rms(o[:, h * GLA_DV:(h + 1) * GLA_DV], gon) for h in range(GLA_HEADS)], axis=1)
        gg = gg_ref[0, r, :]
        og_ref[0, r, :] = (normed * (gg * jax.nn.sigmoid(gg))).astype(BF16)
        return carry

    lax.fori_loop(0, n_chunks, chunk, 0)

    @pl.when(pl.program_id(1) == pl.num_programs(1) - 1)
    def _():
        sfin_ref[0] = st_ref[...]


def _gla(gq, gk, lg, gv, gg, g_on, s0t, lc, tg):
    B, T, _ = gq.shape
    ng = T // tg
    tok = lambda w: pl.BlockSpec((1, tg, w), lambda b, g: (b, g, 0))
    st_spec = pl.BlockSpec((1, GLA_DV, GLA_K), lambda b, g: (b, 0, 0))
    S = jax.ShapeDtypeStruct
    return pl.pallas_call(
        functools.partial(_gla_kernel, lc, tg // lc), grid=(B, ng),
        in_specs=[tok(GLA_K), tok(GLA_K), tok(GLA_K), tok(GLA_V), tok(GLA_V),
                  pl.BlockSpec((1, GLA_DV), lambda b, g: (0, 0)), st_spec],
        out_specs=[tok(GLA_V), st_spec],
        out_shape=[S((B, T, GLA_V), BF16), S((B, GLA_DV, GLA_K), F32)],
        scratch_shapes=[pltpu.VMEM((GLA_DV, GLA_K), F32)],
        compiler_params=pltpu.CompilerParams(dimension_semantics=("parallel", "arbitrary"),
                                             vmem_limit_bytes=VMEM_LIMIT),
        name="gla",
    )(gq, gk, lg, gv, gg, g_on, s0t)


def _ffn_kernel(final, x_ref, om_ref, og_ref, wo_ref, gf_ref, wg_ref, wu_ref, wd_ref, gfin_ref, y_ref):
    cat = jnp.concatenate([om_ref[0], og_ref[0]], axis=1)
    x1 = x_ref[0] + _dot(cat, wo_ref[...])
    hn = _rms(x1, gf_ref[...]).astype(BF16)
    a = _dot(hn, wg_ref[...])
    u = _dot(hn, wu_ref[...])
    act = (a * jax.nn.sigmoid(a) * u).astype(BF16)
    x2 = x1 + _dot(act, wd_ref[...])
    y_ref[0] = _rms(x2, gfin_ref[...]) if final else x2


def _ffn(x, o_mla, o_g, lw, g_final, tm, final):
    B, T, D = x.shape
    tok = lambda w: pl.BlockSpec((1, tm, w), lambda b, t: (b, t, 0))
    const = lambda a: pl.BlockSpec(a.shape, lambda b, t: (0,) * a.ndim, pipeline_mode=pl.Buffered(1))
    weights = (lw["w_o"], lw["g_ffn"], lw["w_g"], lw["w_u"], lw["w_down"], g_final)
    return pl.pallas_call(
        functools.partial(_ffn_kernel, final), grid=(B, T // tm),
        in_specs=[tok(D), tok(o_mla.shape[-1]), tok(o_g.shape[-1])] + [const(w) for w in weights],
        out_specs=tok(D), out_shape=jax.ShapeDtypeStruct((B, T, D), F32),
        compiler_params=pltpu.CompilerParams(dimension_semantics=("parallel", "parallel"),
                                             vmem_limit_bytes=VMEM_LIMIT),
        name="ffn_final" if final else "ffn",
    )(x, o_mla, o_g, *weights)


def _layer_weights(l, g_attn, w_in, g_qn, w_uq, g_kvn, w_ukv, w_a2, b_a2, g_gla_on, w_o, g_ffn, w_gu, w_down):
    d_model = w_in.shape[1]
    wi = w_in[l]
    c = np.cumsum([0, Q_LORA, KV_LORA, QK_ROPE, GLA_K, GLA_K, GLA_V, GLA_V, GATE_LR])
    sm = jnp.concatenate([wi[:, c[2]:c[3]], wi[:, c[7]:c[8]],
                          jnp.zeros((d_model, LANES - QK_ROPE - GATE_LR), F32)], axis=1)
    w_in_r = jnp.concatenate([wi[:, c[0]:c[2]], wi[:, c[3]:c[4]] * (GLA_DK ** -0.5), wi[:, c[4]:c[7]], sm], axis=1)
    wq = w_uq[l].reshape(Q_LORA, MLA_HEADS, QK_NOPE + QK_ROPE)
    w_qn_t = wq[:, :, :QK_NOPE].reshape(Q_LORA, -1).T
    w_qr_t = jnp.concatenate([wq[:, :, QK_NOPE:QK_NOPE + ROPE_HALF].reshape(Q_LORA, -1),
                              wq[:, :, QK_NOPE + ROPE_HALF:].reshape(Q_LORA, -1)], axis=1).T
    wkv = w_ukv[l].reshape(KV_LORA, MLA_HEADS, QK_NOPE + V_HEAD)
    wk_nope = jnp.pad(wkv[:, :, :QK_NOPE], ((0, 0), (0, 0), (0, HEAD_PAD - QK_NOPE))).reshape(KV_LORA, -1)
    place = np.zeros((LANES, MLA_HEADS, HEAD_PAD), np.float32)
    for d in range(QK_ROPE):
        place[d, :, QK_NOPE + d] = 1.0
    w_k = jnp.concatenate([wk_nope, jnp.asarray(place.reshape(LANES, -1))], axis=0)
    w_v_t = wkv[:, :, QK_NOPE:].reshape(KV_LORA, -1).T
    w_a2_p = jnp.zeros((LANES, GLA_K), F32).at[QK_ROPE:QK_ROPE + GATE_LR].set(w_a2[l])
    d_ff = w_down.shape[1]
    row = lambda v: v.reshape(1, -1)
    return {
        "g_attn": row(g_attn[l]), "w_in": w_in_r.astype(BF16),
        "g_qn": row(g_qn[l] * ATTN_SCALE), "w_qnT": w_qn_t.astype(BF16), "w_qrT": w_qr_t.astype(BF16),
        "g_kvn": row(g_kvn[l]), "w_k": w_k.astype(BF16), "w_vT": w_v_t.astype(BF16),
        "w_a2": w_a2_p.astype(BF16), "b_a2": row(b_a2[l]),
        "g_on": row(g_gla_on[l]), "w_o": w_o[l].astype(BF16), "g_ffn": row(g_ffn[l]),
        "w_g": w_gu[l][:, :d_ff].astype(BF16), "w_u": w_gu[l][:, d_ff:].astype(BF16),
        "w_down": w_down[l].astype(BF16),
    }


def _rope_tables(pos):
    inv = ROPE_THETA ** (-jnp.arange(ROPE_HALF, dtype=F32) / ROPE_HALF)
    ang = pos.astype(F32)[:, None] * inv[None, :]
    cos, sin = jnp.cos(ang), jnp.sin(ang)
    zeros = jnp.zeros((pos.shape[0], LANES - QK_ROPE), F32)
    cr = jnp.concatenate([cos, cos, zeros], axis=1)
    sr = jnp.concatenate([-sin, sin, zeros], axis=1)
    ct = jnp.tile(cos, (1, MLA_HEADS)).T
    st = jnp.tile(sin, (1, MLA_HEADS)).T
    return cr, sr, ct, st


def _pick(n, pref):
    for t in pref:
        if n % t == 0:
            return t
    return n


def kernel(x_prompt, x_sample, cache_ckv, cache_krope, state_gla, g_attn, w_in, g_qn, w_uq, g_kvn, w_ukv, w_a2, b_a2,
           g_gla_on, w_o, g_ffn, w_gu, w_down, g_final):
    depth = w_in.shape[0]
    B, T, D = x_prompt.shape
    nb, ts, _ = x_sample.shape
    past = cache_ckv.shape[2]
    assert T % CHUNK == 0 and ts % SUB == 0 and ts <= CHUNK

    tabs_p = _rope_tables(jnp.arange(T, dtype=jnp.int32))
    tabs_s = _rope_tables(jnp.tile(past + jnp.arange(ts, dtype=jnp.int32), nb))
    tm_p = _pick(T, (512, 256, 128))
    tq = tk = _pick(T, (512, 256, 128, 64))
    tg = _pick(T, (512, 256, 128, 64))
    g_fin = g_final.reshape(1, -1)

    xp = x_prompt
    xs = x_sample.reshape(1, nb * ts, D)
    outs = {k: [] for k in ("ckv_p", "kr_p", "st_p", "ckv_s", "kr_s", "st_s")}
    for l in range(depth):
        lw = _layer_weights(l, g_attn, w_in, g_qn, w_uq, g_kvn, w_ukv, w_a2, b_a2, g_gla_on, w_o, g_ffn, w_gu,
                            w_down)
        final = l == depth - 1
        ckv, kr, qt, k, vt, gq, gk, gv, gg, lg = _proj(xp, tabs_p, lw, tm_p, True)
        o_mla = _attn(qt, k, vt, tq, tk, 0)
        s0 = jnp.zeros((B, GLA_DV, GLA_K), F32)
        o_g, st_fin = _gla(gq, gk, lg, gv, gg, lw["g_on"], s0, CHUNK, tg)
        xp = _ffn(xp, o_mla, o_g, lw, g_fin, tm_p, final)
        outs["ckv_p"].append(ckv), outs["kr_p"].append(kr)
        outs["st_p"].append(st_fin.reshape(B, GLA_DV, GLA_HEADS, GLA_DK).transpose(0, 2, 3, 1))
        ckv, kr, qt, gq, gk, gv, gg, lg = _proj(xs, tabs_s, lw, nb * ts, False)
        ckv, kr = ckv.reshape(nb, ts, -1), kr.reshape(nb, ts, -1)
        k, vt = _kvproj(cache_ckv[l], cache_krope[l], ckv, kr, lw)
        qt = qt.reshape(-1, nb, ts).transpose(1, 0, 2)
        o_mla = _attn(qt, k, vt, ts, past + ts, past).reshape(1, nb * ts, -1)
        s0 = state_gla[l].astype(F32).transpose(0, 3, 1, 2).reshape(nb, GLA_DV, GLA_K)
        per_b = lambda a: a.reshape(nb, ts, -1)
        o_g, st_fin = _gla(per_b(gq), per_b(gk), per_b(lg), per_b(gv), per_b(gg), lw["g_on"], s0, ts, ts)
        xs = _ffn(xs, o_mla, o_g.reshape(1, nb * ts, -1), lw, g_fin, nb * ts, final)
        outs["ckv_s"].append(ckv), outs["kr_s"].append(kr)
        outs["st_s"].append(st_fin.reshape(nb, GLA_DV, GLA_HEADS, GLA_DK).transpose(0, 2, 3, 1))

    stack = lambda key: jnp.stack(outs[key])
    return (xp, xs.reshape(nb, ts, D), stack("ckv_p"), stack("kr_p"), stack("st_p"),
            stack("ckv_s"), stack("kr_s"), stack("st_s"))
```

```python
import functools

import jax
import jax.numpy as jnp
import numpy as np
from jax import lax
from jax.experimental import pallas as pl
from jax.experimental.pallas import tpu as pltpu

F32 = jnp.float32
BF16 = jnp.bfloat16

CHUNK = 64
EPS = 1e-6
MLA_HEADS = 8
QK_NOPE = 64
QK_ROPE = 32
ROPE_HALF = QK_ROPE // 2
V_HEAD = 64
Q_LORA = 256
KV_LORA = 256
ROPE_THETA = 10000.0
ATTN_SCALE = (QK_NOPE + QK_ROPE) ** -0.5
LOG2E = 1.4426950408889634
GLA_HEADS = 4
GLA_DK = 64
GLA_DV = 128
GATE_LR = 16
GATE_TAU = 16.0
GLA_K = GLA_HEADS * GLA_DK
GLA_V = GLA_HEADS * GLA_DV

LANES = 128
HEAD_PAD = 128
SUB = 16
NEG = -1e30
VMEM_LIMIT = 48 * 1024 * 1024

Z_CQ = 0
Z_CKV = Z_CQ + Q_LORA
Z_GQ = Z_CKV + KV_LORA
Z_GK = Z_GQ + GLA_K
Z_GV = Z_GK + GLA_K
Z_GG = Z_GV + GLA_V
Z_SM = Z_GG + GLA_V
Z_W = Z_SM + LANES


def _nt(a, b):
    return lax.dot_general(a, b, (((1,), (1,)), ((), ())), preferred_element_type=F32)


def _tn(a, b):
    return lax.dot_general(a, b, (((0,), (0,)), ((), ())), preferred_element_type=F32)


def _dot(a, b):
    return jnp.dot(a, b, preferred_element_type=F32)


def _rms(x, g):
    return x * lax.rsqrt(jnp.mean(x * x, axis=-1, keepdims=True) + EPS) * g


def _proj_kernel(emit_kv, x_ref, cr_ref, sr_ref, ct_ref, st_ref, gat_ref, win_ref, gqn_ref, wqn_ref, wqr_ref,
                 gkv_ref, wk_ref, wvt_ref, wa2_ref, ba2_ref, *outs):
    if emit_kv:
        ckv_ref, kr_ref, qt_ref, k_ref, vt_ref, gq_ref, gk_ref, gv_ref, gg_ref, lg_ref = outs
    else:
        ckv_ref, kr_ref, qt_ref, gq_ref, gk_ref, gv_ref, gg_ref, lg_ref = outs
    x = x_ref[0]
    h = _rms(x, gat_ref[...]).astype(BF16)
    z = _dot(h, win_ref[...])

    cqn = _rms(z[:, Z_CQ:Z_CQ + Q_LORA], gqn_ref[...]).astype(BF16)
    n_nope = MLA_HEADS * QK_NOPE
    n_rope = MLA_HEADS * ROPE_HALF
    qt_ref[0, 0:n_nope, :] = _nt(wqn_ref[...], cqn).astype(BF16)
    qr = _nt(wqr_ref[...], cqn)
    x1, x2 = qr[:n_rope], qr[n_rope:]
    ct, st = ct_ref[...], st_ref[...]
    qt_ref[0, n_nope:n_nope + n_rope, :] = (x1 * ct - x2 * st).astype(BF16)
    qt_ref[0, n_nope + n_rope:, :] = (x1 * st + x2 * ct).astype(BF16)

    ckv = _rms(z[:, Z_CKV:Z_CKV + KV_LORA], gkv_ref[...])
    ckv_ref[0] = ckv
    u = z[:, Z_SM:Z_SM + LANES]
    lane = lax.broadcasted_iota(jnp.int32, u.shape, 1)
    partner = jnp.where(lane < ROPE_HALF, pltpu.roll(u, LANES - ROPE_HALF, 1), pltpu.roll(u, ROPE_HALF, 1))
    kr = u * cr_ref[...] + partner * sr_ref[...]
    kr_ref[0] = kr[:, :QK_ROPE]
    if emit_kv:
        ckv_b = ckv.astype(BF16)
        k_ref[0] = _dot(jnp.concatenate([ckv_b, kr.astype(BF16)], axis=1), wk_ref[...]).astype(BF16)
        vt_ref[0] = _nt(wvt_ref[...], ckv_b).astype(BF16)

    gq_ref[0] = z[:, Z_GQ:Z_GQ + GLA_K]
    gk_ref[0] = z[:, Z_GK:Z_GK + GLA_K]
    gv_ref[0] = z[:, Z_GV:Z_GV + GLA_V].astype(BF16)
    gg_ref[0] = z[:, Z_GG:Z_GG + GLA_V]
    xg = _dot(u.astype(BF16), wa2_ref[...]) + ba2_ref[...]
    lg_ref[0] = (jnp.minimum(xg, 0.0) - jnp.log1p(jnp.exp(-jnp.abs(xg)))) * (1.0 / GATE_TAU)


def _proj(x, tabs, lw, tm, emit_kv):
    B, T, D = x.shape
    nt = T // tm
    cr, sr, ct, st = tabs
    tok = lambda w: pl.BlockSpec((1, tm, w), lambda b, t: (b, t, 0))
    feat = lambda r: pl.BlockSpec((1, r, tm), lambda b, t: (b, 0, t))
    full = lambda a: pl.BlockSpec(a.shape, lambda b, t: (0,) * a.ndim)
    weights = (lw["g_attn"], lw["w_in"], lw["g_qn"], lw["w_qnT"], lw["w_qrT"], lw["g_kvn"], lw["w_k"], lw["w_vT"],
               lw["w_a2"], lw["b_a2"])
    in_specs = [tok(D),
                pl.BlockSpec((tm, LANES), lambda b, t: (t, 0)), pl.BlockSpec((tm, LANES), lambda b, t: (t, 0)),
                pl.BlockSpec((LANES, tm), lambda b, t: (0, t)), pl.BlockSpec((LANES, tm), lambda b, t: (0, t))]
    in_specs += [full(w) for w in weights]
    S = jax.ShapeDtypeStruct
    nq = MLA_HEADS * (QK_NOPE + QK_ROPE)
    out_shape = [S((B, T, KV_LORA), F32), S((B, T, QK_ROPE), F32), S((B, nq, T), BF16)]
    out_specs = [tok(KV_LORA), tok(QK_ROPE), feat(nq)]
    if emit_kv:
        out_shape += [S((B, T, MLA_HEADS * HEAD_PAD), BF16), S((B, MLA_HEADS * V_HEAD, T), BF16)]
        out_specs += [tok(MLA_HEADS * HEAD_PAD), feat(MLA_HEADS * V_HEAD)]
    out_shape += [S((B, T, GLA_K), F32), S((B, T, GLA_K), F32), S((B, T, GLA_V), BF16), S((B, T, GLA_V), F32),
                  S((B, T, GLA_K), F32)]
    out_specs += [tok(GLA_K), tok(GLA_K), tok(GLA_V), tok(GLA_V), tok(GLA_K)]
    return pl.pallas_call(
        functools.partial(_proj_kernel, emit_kv),
        grid=(B, nt), in_specs=in_specs, out_specs=out_specs, out_shape=out_shape,
        compiler_params=pltpu.CompilerParams(dimension_semantics=("parallel", "parallel"),
                                             vmem_limit_bytes=VMEM_LIMIT),
        name="proj_kv" if emit_kv else "proj",
    )(x, cr, sr, ct, st, *weights)


def _kvproj_kernel(cckv_ref, ckr_ref, nckv_ref, nkr_ref, wkc_ref, wkr_ref, wvt_ref, k_ref, vt_ref):
    ckv = jnp.concatenate([cckv_ref[0], nckv_ref[0]], axis=0).astype(BF16)
    kr = jnp.concatenate([ckr_ref[0], nkr_ref[0]], axis=0).astype(BF16)
    k_ref[0] = (_dot(ckv, wkc_ref[...]) + _dot(kr, wkr_ref[...])).astype(BF16)
    vt_ref[0] = _nt(wvt_ref[...], ckv).astype(BF16)


def _kvproj(cache_ckv, cache_kr, new_ckv, new_kr, lw):
    nb, past, _ = cache_ckv.shape
    new = new_ckv.shape[1]
    tot = past + new
    wkc, wkr = lw["w_k"][:KV_LORA], lw["w_k"][KV_LORA:KV_LORA + QK_ROPE]
    per_b = lambda a: pl.BlockSpec((1,) + a.shape[1:], lambda b: (b, 0, 0))
    full = lambda a: pl.BlockSpec(a.shape, lambda b: (0,) * a.ndim)
    S = jax.ShapeDtypeStruct
    return pl.pallas_call(
        _kvproj_kernel, grid=(nb,),
        in_specs=[per_b(cache_ckv), per_b(cache_kr), per_b(new_ckv), per_b(new_kr), full(wkc), full(wkr),
                  full(lw["w_vT"])],
        out_specs=[pl.BlockSpec((1, tot, MLA_HEADS * HEAD_PAD), lambda b: (b, 0, 0)),
                   pl.BlockSpec((1, MLA_HEADS * V_HEAD, tot), lambda b: (b, 0, 0))],
        out_shape=[S((nb, tot, MLA_HEADS * HEAD_PAD), BF16), S((nb, MLA_HEADS * V_HEAD, tot), BF16)],
        compiler_params=pltpu.CompilerParams(dimension_semantics=("parallel",), vmem_limit_bytes=VMEM_LIMIT),
        name="kvproj",
    )(cache_ckv, cache_kr, new_ckv, new_kr, wkc, wkr, lw["w_vT"])


def _attn_kernel(tq, tk, q_off, qi_ref, ki_ref, last_ref, qt_ref, k_ref, vt_ref, o_ref, m_ref, l_ref, acc_ref,
                 s_ref):
    p_id = pl.program_id(1)
    qi, ki = qi_ref[p_id], ki_ref[p_id]
    n_nope = MLA_HEADS * QK_NOPE
    n_rope = MLA_HEADS * ROPE_HALF

    @pl.when(ki == 0)
    def _():
        m_ref[...] = jnp.full(m_ref.shape, NEG, F32)
        l_ref[...] = jnp.zeros(l_ref.shape, F32)
        acc_ref[...] = jnp.zeros(acc_ref.shape, F32)

    rb = LANES if tk % LANES == 0 else tk
    ones_rows = jnp.ones((SUB, rb), BF16)

    def step(masked):
        if masked:
            kpos = ki * tk + lax.broadcasted_iota(jnp.int32, (tk, tq), 0)
            qpos = q_off + qi * tq + lax.broadcasted_iota(jnp.int32, (tk, tq), 1)
            allowed = (kpos // CHUNK) <= (qpos // CHUNK)

        def scores(h):
            qt_h = jnp.concatenate(
                [qt_ref[0, h * QK_NOPE:(h + 1) * QK_NOPE, :],
                 qt_ref[0, n_nope + h * ROPE_HALF:n_nope + (h + 1) * ROPE_HALF, :],
                 qt_ref[0, n_nope + n_rope + h * ROPE_HALF:n_nope + n_rope + (h + 1) * ROPE_HALF, :],
                 jnp.zeros((HEAD_PAD - QK_NOPE - QK_ROPE, tq), BF16)], axis=0)
            s = _dot(k_ref[0, :, h * HEAD_PAD:(h + 1) * HEAD_PAD], qt_h)
            if masked:
                s = jnp.where(allowed, s, NEG)
            s_ref[h % 2] = s
            return jnp.max(s, axis=0, keepdims=True)

        smax_next = scores(0)
        for h in range(MLA_HEADS):
            smax = smax_next
            if h + 1 < MLA_HEADS:
                smax_next = scores(h + 1)
            m_old = m_ref[h:h + 1, :]
            m_new = jnp.maximum(m_old, smax)
            alpha = jnp.exp2(m_old - m_new)
            m_ref[h:h + 1, :] = m_new
            rows = slice(h * V_HEAD, (h + 1) * V_HEAD)
            pv = None
            for r in range(tk // rb):
                p = jnp.exp2(s_ref[h % 2, r * rb:(r + 1) * rb, :] - m_new).astype(BF16)
                v_aug = jnp.concatenate([vt_ref[0, rows, r * rb:(r + 1) * rb], ones_rows], axis=0)
                d = _dot(v_aug, p)
                pv = d if pv is None else pv + d
            acc_ref[rows, :] = alpha * acc_ref[rows, :] + pv[:V_HEAD]
            l_ref[h:h + 1, :] = alpha * l_ref[h:h + 1, :] + pv[V_HEAD:V_HEAD + 1]

    needs_mask = (ki * tk + tk - 1) // CHUNK > (q_off + qi * tq) // CHUNK
    pl.when(needs_mask)(lambda: step(True))
    pl.when(jnp.logical_not(needs_mask))(lambda: step(False))

    @pl.when(last_ref[p_id] == 1)
    def _():
        inv = 1.0 / l_ref[...]
        parts = [acc_ref[h * V_HEAD:(h + 1) * V_HEAD, :] * inv[h:h + 1, :] for h in range(MLA_HEADS)]
        o_ref[0] = jnp.concatenate(parts, axis=0).T.astype(BF16)


def _attn(qt, k, vt, tq, tk, q_off):
    B, nqf, Tq = qt.shape
    Tk = k.shape[1]
    nq, nk = Tq // tq, Tk // tk
    qi_l, ki_l, last_l = [], [], []
    for qi in range(nq):
        last_chunk = (q_off + qi * tq + tq - 1) // CHUNK
        kmax = min(nk - 1, ((last_chunk + 1) * CHUNK - 1) // tk)
        for ki in range(kmax + 1):
            qi_l.append(qi), ki_l.append(ki), last_l.append(int(ki == kmax))
    tabs = [jnp.asarray(np.asarray(a, np.int32)) for a in (qi_l, ki_l, last_l)]
    grid_spec = pltpu.PrefetchScalarGridSpec(
        num_scalar_prefetch=3, grid=(B, len(qi_l)),
        in_specs=[pl.BlockSpec((1, nqf, tq), lambda b, p, qi, ki, la: (b, 0, qi[p])),
                  pl.BlockSpec((1, tk, MLA_HEADS * HEAD_PAD), lambda b, p, qi, ki, la: (b, ki[p], 0)),
                  pl.BlockSpec((1, MLA_HEADS * V_HEAD, tk), lambda b, p, qi, ki, la: (b, 0, ki[p]))],
        out_specs=pl.BlockSpec((1, tq, MLA_HEADS * V_HEAD), lambda b, p, qi, ki, la: (b, qi[p], 0)),
        scratch_shapes=[pltpu.VMEM((MLA_HEADS, tq), F32), pltpu.VMEM((MLA_HEADS, tq), F32),
                        pltpu.VMEM((MLA_HEADS * V_HEAD, tq), F32), pltpu.VMEM((2, tk, tq), F32)])
    return pl.pallas_call(
        functools.partial(_attn_kernel, tq, tk, q_off), grid_spec=grid_spec,
        out_shape=jax.ShapeDtypeStruct((B, Tq, MLA_HEADS * V_HEAD), BF16),
        compiler_params=pltpu.CompilerParams(dimension_semantics=("parallel", "arbitrary"),
                                             vmem_limit_bytes=VMEM_LIMIT),
        name="attn",
    )(*tabs, qt, k, vt)


def _gla_kernel(lc, n_chunks, gq_ref, gk_ref, lg_ref, gv_ref, gg_ref, gon_ref, s0_ref, og_ref, sfin_ref, st_ref):
    n_sub = lc // SUB
    n_stack = SUB * n_sub * (n_sub - 1) // 2
    stack_pad = max(LANES, -(-n_stack // LANES) * LANES)

    @pl.when(pl.program_id(1) == 0)
    def _():
        st_ref[...] = s0_ref[0]

    lane_k = lax.broadcasted_iota(jnp.int32, (lc, GLA_K), 1) // GLA_DK
    ri = lax.broadcasted_iota(jnp.int32, (lc, lc), 0)
    ci = lax.broadcasted_iota(jnp.int32, (lc, lc), 1)
    tri = (ci <= ri).astype(BF16)
    e_r = lax.broadcasted_iota(jnp.int32, (GLA_K, GLA_V), 0) // GLA_DK
    e_c = lax.broadcasted_iota(jnp.int32, (GLA_K, GLA_V), 1) // GLA_DV
    e_mat = (e_r == e_c).astype(BF16)
    sub_row = lax.broadcasted_iota(jnp.int32, (SUB, GLA_K), 0)
    st_lane = lax.broadcasted_iota(jnp.int32, (GLA_DV, GLA_K), 1) // GLA_DK
    a_row = lax.broadcasted_iota(jnp.int32, (lc, stack_pad), 0) // SUB
    a_col = lax.broadcasted_iota(jnp.int32, (lc, stack_pad), 1)
    off_mask = jnp.zeros((lc, stack_pad), jnp.bool_)
    seg = 0
    for i_sub in range(1, n_sub):
        off_mask = off_mask | ((a_row == i_sub) & (a_col >= seg) & (a_col < seg + SUB * i_sub))
        seg += SUB * i_sub

    def heads_on_rows(a):
        return jnp.concatenate([jnp.where(lane_k == h, a, 0.0) for h in range(GLA_HEADS)], axis=0).astype(BF16)

    def chunk(c, carry):
        r = pl.ds(pl.multiple_of(c * lc, lc), lc)
        q, k, lg = gq_ref[0, r, :], gk_ref[0, r, :], lg_ref[0, r, :]
        v_b = gv_ref[0, r, :]
        v_f = v_b.astype(F32)
        hi = lg.astype(BF16)
        r1 = lg - hi.astype(F32)
        mid = r1.astype(BF16)
        lo = (r1 - mid.astype(F32)).astype(BF16)
        b = _dot(tri, hi) + _dot(tri, mid) + _dot(tri, lo)
        b_last = b[lc - 1:lc, :]
        st = st_ref[...]

        o_heads = _nt(heads_on_rows(q * jnp.exp(b)), st.astype(BF16))

        kd = (k * jnp.exp(b_last - b)).astype(BF16)
        upd = jnp.zeros((GLA_DV, GLA_K), F32)
        for h in range(GLA_HEADS):
            upd = upd + jnp.where(st_lane == h, _tn(v_b[:, h * GLA_DV:(h + 1) * GLA_DV], kd), 0.0)
        st_ref[...] = jnp.exp(b_last) * st + upd

        o_off = [None] * GLA_HEADS
        if n_sub > 1:
            refs = [jnp.zeros((SUB, GLA_K), F32)]
            kst, vst = [], []
            for i_sub in range(1, n_sub):
                r_i = b[SUB * i_sub - 1:SUB * i_sub, :]
                refs.append(jnp.broadcast_to(r_i, (SUB, GLA_K)))
                kst.append(k[:SUB * i_sub] * jnp.exp(r_i - b[:SUB * i_sub]))
                vst.append(v_b[:SUB * i_sub])
            if stack_pad > n_stack:
                kst.append(jnp.zeros((stack_pad - n_stack, GLA_K), F32))
                vst.append(jnp.zeros((stack_pad - n_stack, GLA_V), BF16))
            kst = jnp.concatenate(kst, axis=0).astype(BF16)
            vst = jnp.concatenate(vst, axis=0)
            qp = q * jnp.exp(b - jnp.concatenate(refs, axis=0))
            a4 = _nt(heads_on_rows(qp), kst)
            for h in range(GLA_HEADS):
                a_h = jnp.where(off_mask, a4[h * lc:(h + 1) * lc], 0.0).astype(BF16)
                o_off[h] = _dot(a_h, vst[:, h * GLA_DV:(h + 1) * GLA_DV])

        o = jnp.concatenate(
            [o_heads[h * lc:(h + 1) * lc] + (o_off[h] if n_sub > 1 else 0.0) for h in range(GLA_HEADS)], axis=1)

        o_diag = []
        for i_sub in range(n_sub):
            rs = slice(SUB * i_sub, SUB * (i_sub + 1))
            q_i, k_i, b_i, v_i = q[rs], k[rs], b[rs], v_f[rs]
            terms = []
            for j in range(SUB):
                arg = jnp.where(sub_row >= j, b_i - b_i[j:j + 1], NEG)
                terms.append(q_i * (k_i[j:j + 1] * jnp.exp(arg)))
            w = _dot(jnp.concatenate(terms, axis=0).astype(BF16), e_mat)
            acc = w[0:SUB] * v_i[0:1]
            for j in range(1, SUB):
                acc = acc + w[j * SUB:(j + 1) * SUB] * v_i[j:j + 1]
            o_diag.append(acc)
        o = o + jnp.concatenate(o_diag, axis=0)

        gon = gon_ref[...]
        normed = jnp.concatenate(
            [_rms(o[:, h * GLA_DV:(h + 1) * GLA_DV], gon) for h in range(GLA_HEADS)], axis=1)
        gg = gg_ref[0, r, :]
        og_ref[0, r, :] = (normed * (gg * jax.nn.sigmoid(gg))).astype(BF16)
        return carry

    lax.fori_loop(0, n_chunks, chunk, 0)

    @pl.when(pl.program_id(1) == pl.num_programs(1) - 1)
    def _():
        sfin_ref[0] = st_ref[...]


def _gla(gq, gk, lg, gv, gg, g_on, s0t, lc, tg):
    B, T, _ = gq.shape
    ng = T // tg
    tok = lambda w: pl.BlockSpec((1, tg, w), lambda b, g: (b, g, 0))
    st_spec = pl.BlockSpec((1, GLA_DV, GLA_K), lambda b, g: (b, 0, 0))
    S = jax.ShapeDtypeStruct
    return pl.pallas_call(
        functools.partial(_gla_kernel, lc, tg // lc), grid=(B, ng),
        in_specs=[tok(GLA_K), tok(GLA_K), tok(GLA_K), tok(GLA_V), tok(GLA_V),
                  pl.BlockSpec((1, GLA_DV), lambda b, g: (0, 0)), st_spec],
        out_specs=[tok(GLA_V), st_spec],
        out_shape=[S((B, T, GLA_V), BF16), S((B, GLA_DV, GLA_K), F32)],
        scratch_shapes=[pltpu.VMEM((GLA_DV, GLA_K), F32)],
        compiler_params=pltpu.CompilerParams(dimension_semantics=("parallel", "arbitrary"),
                                             vmem_limit_bytes=VMEM_LIMIT),
        name="gla",
    )(gq, gk, lg, gv, gg, g_on, s0t)


def _ffn_kernel(final, x_ref, om_ref, og_ref, wo_ref, gf_ref, wg_ref, wu_ref, wd_ref, gfin_ref, y_ref):
    cat = jnp.concatenate([om_ref[0], og_ref[0]], axis=1)
    x1 = x_ref[0] + _dot(cat, wo_ref[...])
    hn = _rms(x1, gf_ref[...]).astype(BF16)
    a = _dot(hn, wg_ref[...])
    u = _dot(hn, wu_ref[...])
    act = (a * jax.nn.sigmoid(a) * u).astype(BF16)
    x2 = x1 + _dot(act, wd_ref[...])
    y_ref[0] = _rms(x2, gfin_ref[...]) if final else x2


def _ffn(x, o_mla, o_g, lw, g_final, tm, final):
    B, T, D = x.shape
    tok = lambda w: pl.BlockSpec((1, tm, w), lambda b, t: (b, t, 0))
    const = lambda a: pl.BlockSpec(a.shape, lambda b, t: (0,) * a.ndim, pipeline_mode=pl.Buffered(1))
    weights = (lw["w_o"], lw["g_ffn"], lw["w_g"], lw["w_u"], lw["w_down"], g_final)
    return pl.pallas_call(
        functools.partial(_ffn_kernel, final), grid=(B, T // tm),
        in_specs=[tok(D), tok(o_mla.shape[-1]), tok(o_g.shape[-1])] + [const(w) for w in weights],
        out_specs=tok(D), out_shape=jax.ShapeDtypeStruct((B, T, D), F32),
        compiler_params=pltpu.CompilerParams(dimension_semantics=("parallel", "parallel"),
                                             vmem_limit_bytes=VMEM_LIMIT),
        name="ffn_final" if final else "ffn",
    )(x, o_mla, o_g, *weights)


def _layer_weights(l, g_attn, w_in, g_qn, w_uq, g_kvn, w_ukv, w_a2, b_a2, g_gla_on, w_o, g_ffn, w_gu, w_down):
    d_model = w_in.shape[1]
    wi = w_in[l]
    c = np.cumsum([0, Q_LORA, KV_LORA, QK_ROPE, GLA_K, GLA_K, GLA_V, GLA_V, GATE_LR])
    sm = jnp.concatenate([wi[:, c[2]:c[3]], wi[:, c[7]:c[8]],
                          jnp.zeros((d_model, LANES - QK_ROPE - GATE_LR), F32)], axis=1)
    w_in_r = jnp.concatenate([wi[:, c[0]:c[2]], wi[:, c[3]:c[4]] * (GLA_DK ** -0.5), wi[:, c[4]:c[7]], sm], axis=1)
    wq = w_uq[l].reshape(Q_LORA, MLA_HEADS, QK_NOPE + QK_ROPE)
    w_qn_t = wq[:, :, :QK_NOPE].reshape(Q_LORA, -1).T
    w_qr_t = jnp.concatenate([wq[:, :, QK_NOPE:QK_NOPE + ROPE_HALF].reshape(Q_LORA, -1),
                              wq[:, :, QK_NOPE + ROPE_HALF:].reshape(Q_LORA, -1)], axis=1).T
    wkv = w_ukv[l].reshape(KV_LORA, MLA_HEADS, QK_NOPE + V_HEAD)
    wk_nope = jnp.pad(wkv[:, :, :QK_NOPE], ((0, 0), (0, 0), (0, HEAD_PAD - QK_NOPE))).reshape(KV_LORA, -1)
    place = np.zeros((LANES, MLA_HEADS, HEAD_PAD), np.float32)
    for d in range(QK_ROPE):
        place[d, :, QK_NOPE + d] = 1.0
    w_k = jnp.concatenate([wk_nope, jnp.asarray(place.reshape(LANES, -1))], axis=0)
    w_v_t = wkv[:, :, QK_NOPE:].reshape(KV_LORA, -1).T
    w_a2_p = jnp.zeros((LANES, GLA_K), F32).at[QK_ROPE:QK_ROPE + GATE_LR].set(w_a2[l])
    d_ff = w_down.shape[1]
    row = lambda v: v.reshape(1, -1)
    return {
        "g_attn": row(g_attn[l]), "w_in": w_in_r.astype(BF16),
        "g_qn": row(g_qn[l] * (ATTN_SCALE * LOG2E)), "w_qnT": w_qn_t.astype(BF16), "w_qrT": w_qr_t.astype(BF16),
        "g_kvn": row(g_kvn[l]), "w_k": w_k.astype(BF16), "w_vT": w_v_t.astype(BF16),
        "w_a2": w_a2_p.astype(BF16), "b_a2": row(b_a2[l]),
        "g_on": row(g_gla_on[l]), "w_o": w_o[l].astype(BF16), "g_ffn": row(g_ffn[l]),
        "w_g": w_gu[l][:, :d_ff].astype(BF16), "w_u": w_gu[l][:, d_ff:].astype(BF16),
        "w_down": w_down[l].astype(BF16),
    }


def _rope_tables(pos):
    inv = ROPE_THETA ** (-jnp.arange(ROPE_HALF, dtype=F32) / ROPE_HALF)
    ang = pos.astype(F32)[:, None] * inv[None, :]
    cos, sin = jnp.cos(ang), jnp.sin(ang)
    zeros = jnp.zeros((pos.shape[0], LANES - QK_ROPE), F32)
    cr = jnp.concatenate([cos, cos, zeros], axis=1)
    sr = jnp.concatenate([-sin, sin, zeros], axis=1)
    ct = jnp.tile(cos, (1, MLA_HEADS)).T
    st = jnp.tile(sin, (1, MLA_HEADS)).T
    return cr, sr, ct, st


def _pick(n, pref):
    for t in pref:
        if n % t == 0:
            return t
    return n


def kernel(x_prompt, x_sample, cache_ckv, cache_krope, state_gla, g_attn, w_in, g_qn, w_uq, g_kvn, w_ukv, w_a2, b_a2,
           g_gla_on, w_o, g_ffn, w_gu, w_down, g_final):
    depth = w_in.shape[0]
    B, T, D = x_prompt.shape
    nb, ts, _ = x_sample.shape
    past = cache_ckv.shape[2]
    assert T % CHUNK == 0 and ts % SUB == 0 and ts <= CHUNK

    tabs_p = _rope_tables(jnp.arange(T, dtype=jnp.int32))
    tabs_s = _rope_tables(jnp.tile(past + jnp.arange(ts, dtype=jnp.int32), nb))
    tm_p = _pick(T, (512, 256, 128))
    tq = tk = _pick(T, (512, 256, 128, 64))
    tg = _pick(T, (512, 256, 128, 64))
    g_fin = g_final.reshape(1, -1)

    xp = x_prompt
    xs = x_sample.reshape(1, nb * ts, D)
    outs = {k: [] for k in ("ckv_p", "kr_p", "st_p", "ckv_s", "kr_s", "st_s")}
    for l in range(depth):
        lw = _layer_weights(l, g_attn, w_in, g_qn, w_uq, g_kvn, w_ukv, w_a2, b_a2, g_gla_on, w_o, g_ffn, w_gu,
                            w_down)
        final = l == depth - 1
        ckv, kr, qt, k, vt, gq, gk, gv, gg, lg = _proj(xp, tabs_p, lw, tm_p, True)
        o_mla = _attn(qt, k, vt, tq, tk, 0)
        s0 = jnp.zeros((B, GLA_DV, GLA_K), F32)
        o_g, st_fin = _gla(gq, gk, lg, gv, gg, lw["g_on"], s0, CHUNK, tg)
        xp = _ffn(xp, o_mla, o_g, lw, g_fin, tm_p, final)
        outs["ckv_p"].append(ckv), outs["kr_p"].append(kr)
        outs["st_p"].append(st_fin.reshape(B, GLA_DV, GLA_HEADS, GLA_DK).transpose(0, 2, 3, 1))
        ckv, kr, qt, gq, gk, gv, gg, lg = _proj(xs, tabs_s, lw, nb * ts, False)
        ckv, kr = ckv.reshape(nb, ts, -1), kr.reshape(nb, ts, -1)
        k, vt = _kvproj(cache_ckv[l], cache_krope[l], ckv, kr, lw)
        qt = qt.reshape(-1, nb, ts).transpose(1, 0, 2)
        o_mla = _attn(qt, k, vt, ts, past + ts, past).reshape(1, nb * ts, -1)
        s0 = state_gla[l].astype(F32).transpose(0, 3, 1, 2).reshape(nb, GLA_DV, GLA_K)
        per_b = lambda a: a.reshape(nb, ts, -1)
        o_g, st_fin = _gla(per_b(gq), per_b(gk), per_b(lg), per_b(gv), per_b(gg), lw["g_on"], s0, ts, ts)
        xs = _ffn(xs, o_mla, o_g.reshape(1, nb * ts, -1), lw, g_fin, nb * ts, final)
        outs["ckv_s"].append(ckv), outs["kr_s"].append(kr)
        outs["st_s"].append(st_fin.reshape(nb, GLA_DV, GLA_HEADS, GLA_DK).transpose(0, 2, 3, 1))

    stack = lambda key: jnp.stack(outs[key])
    return (xp, xs.reshape(nb, ts, D), stack("ckv_p"), stack("kr_p"), stack("st_p"),
            stack("ckv_s"), stack("kr_s"), stack("st_s"))
```

```python
import functools

import jax
import jax.numpy as jnp
import numpy as np
from jax import lax
from jax.experimental import pallas as pl
from jax.experimental.pallas import tpu as pltpu

F32 = jnp.float32
BF16 = jnp.bfloat16

CHUNK = 64
EPS = 1e-6
MLA_HEADS = 8
QK_NOPE = 64
QK_ROPE = 32
ROPE_HALF = QK_ROPE // 2
V_HEAD = 64
Q_LORA = 256
KV_LORA = 256
ROPE_THETA = 10000.0
ATTN_SCALE = (QK_NOPE + QK_ROPE) ** -0.5
LOG2E = 1.4426950408889634
GLA_HEADS = 4
GLA_DK = 64
GLA_DV = 128
GATE_LR = 16
GATE_TAU = 16.0
GLA_K = GLA_HEADS * GLA_DK
GLA_V = GLA_HEADS * GLA_DV

LANES = 128
BF16_ROWS = 16
HEAD_PAD = 128
PV_ROWS = 256
SUB = 8
NEG = -1e30
VMEM_LIMIT = 48 * 1024 * 1024

Z_CQ = 0
Z_CKV = Z_CQ + Q_LORA
Z_GQ = Z_CKV + KV_LORA
Z_GK = Z_GQ + GLA_K
Z_GV = Z_GK + GLA_K
Z_GG = Z_GV + GLA_V
Z_SM = Z_GG + GLA_V
Z_W = Z_SM + LANES


def _nt(a, b):
    return lax.dot_general(a, b, (((1,), (1,)), ((), ())), preferred_element_type=F32)


def _tn(a, b):
    return lax.dot_general(a, b, (((0,), (0,)), ((), ())), preferred_element_type=F32)


def _dot(a, b):
    return jnp.dot(a, b, preferred_element_type=F32)


def _rms(x, g):
    return x * lax.rsqrt(jnp.mean(x * x, axis=-1, keepdims=True) + EPS) * g


def _proj_kernel(emit_kv, x_ref, cr_ref, sr_ref, ct_ref, st_ref, gat_ref, win_ref, gqn_ref, wqn_ref, wqr_ref,
                 gkv_ref, wk_ref, wvt_ref, wa2_ref, ba2_ref, *outs):
    if emit_kv:
        ckv_ref, kr_ref, qt_ref, k_ref, vt_ref, gq_ref, gk_ref, gv_ref, gg_ref, lg_ref = outs
    else:
        ckv_ref, kr_ref, qt_ref, gq_ref, gk_ref, gv_ref, gg_ref, lg_ref = outs
    x = x_ref[0]
    h = _rms(x, gat_ref[...]).astype(BF16)
    z = _dot(h, win_ref[...])

    cqn = _rms(z[:, Z_CQ:Z_CQ + Q_LORA], gqn_ref[...]).astype(BF16)
    n_nope = MLA_HEADS * QK_NOPE
    n_rope = MLA_HEADS * ROPE_HALF
    qt_ref[0, 0:n_nope, :] = _nt(wqn_ref[...], cqn).astype(BF16)
    qr = _nt(wqr_ref[...], cqn)
    x1, x2 = qr[:n_rope], qr[n_rope:]
    ct, st = ct_ref[...], st_ref[...]
    qt_ref[0, n_nope:n_nope + n_rope, :] = (x1 * ct - x2 * st).astype(BF16)
    qt_ref[0, n_nope + n_rope:, :] = (x1 * st + x2 * ct).astype(BF16)

    ckv = _rms(z[:, Z_CKV:Z_CKV + KV_LORA], gkv_ref[...])
    ckv_ref[0] = ckv
    u = z[:, Z_SM:Z_SM + LANES]
    lane = lax.broadcasted_iota(jnp.int32, u.shape, 1)
    partner = jnp.where(lane < ROPE_HALF, pltpu.roll(u, LANES - ROPE_HALF, 1), pltpu.roll(u, ROPE_HALF, 1))
    kr = u * cr_ref[...] + partner * sr_ref[...]
    kr_ref[0] = kr[:, :QK_ROPE]
    if emit_kv:
        ckv_b = ckv.astype(BF16)
        k_ref[0] = _dot(jnp.concatenate([ckv_b, kr.astype(BF16)], axis=1), wk_ref[...]).astype(BF16)
        vt_ref[0] = _nt(wvt_ref[...], ckv_b).astype(BF16)

    gq_ref[0] = z[:, Z_GQ:Z_GQ + GLA_K]
    gk_ref[0] = z[:, Z_GK:Z_GK + GLA_K]
    gv_ref[0] = z[:, Z_GV:Z_GV + GLA_V].astype(BF16)
    gg_ref[0] = z[:, Z_GG:Z_GG + GLA_V]
    xg = _dot(u.astype(BF16), wa2_ref[...]) + ba2_ref[...]
    lg_ref[0] = (jnp.minimum(xg, 0.0) - jnp.log1p(jnp.exp(-jnp.abs(xg)))) * (1.0 / GATE_TAU)


def _proj(x, tabs, lw, tm, emit_kv):
    B, T, D = x.shape
    nt = T // tm
    cr, sr, ct, st = tabs
    tok = lambda w: pl.BlockSpec((1, tm, w), lambda b, t: (b, t, 0))
    feat = lambda r: pl.BlockSpec((1, r, tm), lambda b, t: (b, 0, t))
    full = lambda a: pl.BlockSpec(a.shape, lambda b, t: (0,) * a.ndim)
    weights = (lw["g_attn"], lw["w_in"], lw["g_qn"], lw["w_qnT"], lw["w_qrT"], lw["g_kvn"], lw["w_k"], lw["w_vT"],
               lw["w_a2"], lw["b_a2"])
    in_specs = [tok(D),
                pl.BlockSpec((tm, LANES), lambda b, t: (t, 0)), pl.BlockSpec((tm, LANES), lambda b, t: (t, 0)),
                pl.BlockSpec((LANES, tm), lambda b, t: (0, t)), pl.BlockSpec((LANES, tm), lambda b, t: (0, t))]
    in_specs += [full(w) for w in weights]
    S = jax.ShapeDtypeStruct
    nq = MLA_HEADS * (QK_NOPE + QK_ROPE)
    out_shape = [S((B, T, KV_LORA), F32), S((B, T, QK_ROPE), F32), S((B, nq, T), BF16)]
    out_specs = [tok(KV_LORA), tok(QK_ROPE), feat(nq)]
    if emit_kv:
        out_shape += [S((B, T, MLA_HEADS * HEAD_PAD), BF16), S((B, MLA_HEADS * V_HEAD, T), BF16)]
        out_specs += [tok(MLA_HEADS * HEAD_PAD), feat(MLA_HEADS * V_HEAD)]
    out_shape += [S((B, T, GLA_K), F32), S((B, T, GLA_K), F32), S((B, T, GLA_V), BF16), S((B, T, GLA_V), F32),
                  S((B, T, GLA_K), F32)]
    out_specs += [tok(GLA_K), tok(GLA_K), tok(GLA_V), tok(GLA_V), tok(GLA_K)]
    return pl.pallas_call(
        functools.partial(_proj_kernel, emit_kv),
        grid=(B, nt), in_specs=in_specs, out_specs=out_specs, out_shape=out_shape,
        compiler_params=pltpu.CompilerParams(dimension_semantics=("parallel", "parallel"),
                                             vmem_limit_bytes=VMEM_LIMIT),
        name="proj_kv" if emit_kv else "proj",
    )(x, cr, sr, ct, st, *weights)


def _kvproj_kernel(cckv_ref, ckr_ref, nckv_ref, nkr_ref, wkc_ref, wkr_ref, wvt_ref, k_ref, vt_ref):
    ckv = jnp.concatenate([cckv_ref[0], nckv_ref[0]], axis=0).astype(BF16)
    kr = jnp.concatenate([ckr_ref[0], nkr_ref[0]], axis=0).astype(BF16)
    k_ref[0] = (_dot(ckv, wkc_ref[...]) + _dot(kr, wkr_ref[...])).astype(BF16)
    vt_ref[0] = _nt(wvt_ref[...], ckv).astype(BF16)


def _kvproj(cache_ckv, cache_kr, new_ckv, new_kr, lw):
    nb, past, _ = cache_ckv.shape
    new = new_ckv.shape[1]
    tot = past + new
    wkc, wkr = lw["w_k"][:KV_LORA], lw["w_k"][KV_LORA:KV_LORA + QK_ROPE]
    per_b = lambda a: pl.BlockSpec((1,) + a.shape[1:], lambda b: (b, 0, 0))
    full = lambda a: pl.BlockSpec(a.shape, lambda b: (0,) * a.ndim)
    S = jax.ShapeDtypeStruct
    return pl.pallas_call(
        _kvproj_kernel, grid=(nb,),
        in_specs=[per_b(cache_ckv), per_b(cache_kr), per_b(new_ckv), per_b(new_kr), full(wkc), full(wkr),
                  full(lw["w_vT"])],
        out_specs=[pl.BlockSpec((1, tot, MLA_HEADS * HEAD_PAD), lambda b: (b, 0, 0)),
                   pl.BlockSpec((1, MLA_HEADS * V_HEAD, tot), lambda b: (b, 0, 0))],
        out_shape=[S((nb, tot, MLA_HEADS * HEAD_PAD), BF16), S((nb, MLA_HEADS * V_HEAD, tot), BF16)],
        compiler_params=pltpu.CompilerParams(dimension_semantics=("parallel",), vmem_limit_bytes=VMEM_LIMIT),
        name="kvproj",
    )(cache_ckv, cache_kr, new_ckv, new_kr, wkc, wkr, lw["w_vT"])


def _attn_kernel(tq, tk, sw, q_off, qi_ref, ki_ref, last_ref, qt_ref, k_ref, vt_ref, o_ref, m_ref, l_ref, acc_ref,
                 s_ref):
    p_id = pl.program_id(1)
    qi, ki = qi_ref[p_id], ki_ref[p_id]
    n_nope = MLA_HEADS * QK_NOPE
    n_rope = MLA_HEADS * ROPE_HALF

    @pl.when(ki == 0)
    def _():
        m_ref[...] = jnp.full(m_ref.shape, NEG, F32)
        l_ref[...] = jnp.zeros(l_ref.shape, F32)
        acc_ref[...] = jnp.zeros(acc_ref.shape, F32)

    rb = PV_ROWS if tk % PV_ROWS == 0 else tk
    ones_rows = jnp.ones((BF16_ROWS, rb), BF16)

    n_strips = tq // sw

    def step(modes):
        stages = [(h, i) for h in range(MLA_HEADS) for i in range(n_strips) if modes[i] != "skip"]

        def scores(t):
            h, i = stages[t]
            cols = slice(i * sw, (i + 1) * sw)
            qt_h = jnp.concatenate(
                [qt_ref[0, h * QK_NOPE:(h + 1) * QK_NOPE, cols],
                 qt_ref[0, n_nope + h * ROPE_HALF:n_nope + (h + 1) * ROPE_HALF, cols],
                 qt_ref[0, n_nope + n_rope + h * ROPE_HALF:n_nope + n_rope + (h + 1) * ROPE_HALF, cols],
                 jnp.zeros((HEAD_PAD - QK_NOPE - QK_ROPE, sw), BF16)], axis=0)
            s = _dot(k_ref[0, :, h * HEAD_PAD:(h + 1) * HEAD_PAD], qt_h)
            if modes[i] == "masked":
                kpos = ki * tk + lax.broadcasted_iota(jnp.int32, (tk, sw), 0)
                qpos = q_off + qi * tq + i * sw + lax.broadcasted_iota(jnp.int32, (tk, sw), 1)
                s = jnp.where((kpos // CHUNK) <= (qpos // CHUNK), s, NEG)
            s_ref[t % 2] = s
            return jnp.max(s, axis=0, keepdims=True)

        smax_next = scores(0)
        for t, (h, i) in enumerate(stages):
            smax = smax_next
            if t + 1 < len(stages):
                smax_next = scores(t + 1)
            cols = slice(i * sw, (i + 1) * sw)
            m_old = m_ref[h:h + 1, cols]
            m_new = jnp.maximum(m_old, smax)
            alpha = jnp.exp2(m_old - m_new)
            m_ref[h:h + 1, cols] = m_new
            rows = slice(h * V_HEAD, (h + 1) * V_HEAD)
            pv = None
            for r in range(tk // rb):
                p = jnp.exp2(s_ref[t % 2, r * rb:(r + 1) * rb, :] - m_new).astype(BF16)
                v_aug = jnp.concatenate([vt_ref[0, rows, r * rb:(r + 1) * rb], ones_rows], axis=0)
                d = _dot(v_aug, p)
                pv = d if pv is None else pv + d
            acc_ref[rows, cols] = alpha * acc_ref[rows, cols] + pv[:V_HEAD]
            l_ref[h:h + 1, cols] = alpha * l_ref[h:h + 1, cols] + pv[V_HEAD:V_HEAD + 1]

    if tk == sw and q_off % sw == 0:
        rel = ki - (q_off // sw + qi * n_strips)
        pl.when(rel < 0)(lambda: step(["full"] * n_strips))
        for r in range(n_strips):
            modes = ["skip"] * r + ["masked"] + ["full"] * (n_strips - r - 1)
            pl.when(rel == r)(functools.partial(step, modes))
    else:
        step(["masked"] * n_strips)

    @pl.when(last_ref[p_id] == 1)
    def _():
        inv = 1.0 / l_ref[...]
        parts = [acc_ref[h * V_HEAD:(h + 1) * V_HEAD, :] * inv[h:h + 1, :] for h in range(MLA_HEADS)]
        o_ref[0] = jnp.concatenate(parts, axis=0).T.astype(BF16)


def _attn(qt, k, vt, tq, tk, sw, q_off):
    B, nqf, Tq = qt.shape
    Tk = k.shape[1]
    nq, nk = Tq // tq, Tk // tk
    qi_l, ki_l, last_l = [], [], []
    for qi in range(nq):
        last_chunk = (q_off + qi * tq + tq - 1) // CHUNK
        kmax = min(nk - 1, ((last_chunk + 1) * CHUNK - 1) // tk)
        for ki in range(kmax + 1):
            qi_l.append(qi), ki_l.append(ki), last_l.append(int(ki == kmax))
    tabs = [jnp.asarray(np.asarray(a, np.int32)) for a in (qi_l, ki_l, last_l)]
    grid_spec = pltpu.PrefetchScalarGridSpec(
        num_scalar_prefetch=3, grid=(B, len(qi_l)),
        in_specs=[pl.BlockSpec((1, nqf, tq), lambda b, p, qi, ki, la: (b, 0, qi[p])),
                  pl.BlockSpec((1, tk, MLA_HEADS * HEAD_PAD), lambda b, p, qi, ki, la: (b, ki[p], 0)),
                  pl.BlockSpec((1, MLA_HEADS * V_HEAD, tk), lambda b, p, qi, ki, la: (b, 0, ki[p]))],
        out_specs=pl.BlockSpec((1, tq, MLA_HEADS * V_HEAD), lambda b, p, qi, ki, la: (b, qi[p], 0)),
        scratch_shapes=[pltpu.VMEM((MLA_HEADS, tq), F32), pltpu.VMEM((MLA_HEADS, tq), F32),
                        pltpu.VMEM((MLA_HEADS * V_HEAD, tq), F32), pltpu.VMEM((2, tk, sw), F32)])
    return pl.pallas_call(
        functools.partial(_attn_kernel, tq, tk, sw, q_off), grid_spec=grid_spec,
        out_shape=jax.ShapeDtypeStruct((B, Tq, MLA_HEADS * V_HEAD), BF16),
        compiler_params=pltpu.CompilerParams(dimension_semantics=("parallel", "arbitrary"),
                                             vmem_limit_bytes=VMEM_LIMIT),
        name="attn",
    )(*tabs, qt, k, vt)


def _gla_kernel(lc, n_chunks, nch, gq_ref, gk_ref, lg_ref, gv_ref, gg_ref, gon_ref, s0_ref, og_ref, sfin_ref,
                st_ref):
    n_sub = lc // SUB
    n_stack = SUB * n_sub * (n_sub - 1) // 2
    stack_pad = -(-n_stack // LANES) * LANES

    @pl.when(pl.program_id(1) == 0)
    def _():
        st_ref[...] = s0_ref[0]

    lane_k = lax.broadcasted_iota(jnp.int32, (lc, GLA_K), 1) // GLA_DK
    ri = lax.broadcasted_iota(jnp.int32, (lc, lc), 0)
    ci = lax.broadcasted_iota(jnp.int32, (lc, lc), 1)
    tri = (ci <= ri).astype(BF16)
    e_r = lax.broadcasted_iota(jnp.int32, (GLA_K, GLA_V), 0) // GLA_DK
    e_c = lax.broadcasted_iota(jnp.int32, (GLA_K, GLA_V), 1) // GLA_DV
    e_mat = (e_r == e_c).astype(BF16)
    sub_row = lax.broadcasted_iota(jnp.int32, (SUB, GLA_K), 0)
    st_lane = lax.broadcasted_iota(jnp.int32, (GLA_DV, GLA_K), 1) // GLA_DK
    a_row = lax.broadcasted_iota(jnp.int32, (lc, stack_pad), 0) // SUB
    a_col = lax.broadcasted_iota(jnp.int32, (lc, stack_pad), 1)
    off_mask = jnp.zeros((lc, stack_pad), jnp.bool_)
    seg = 0
    for i_sub in range(1, n_sub):
        off_mask = off_mask | ((a_row == i_sub) & (a_col >= seg) & (a_col < seg + SUB * i_sub))
        seg += SUB * i_sub

    def heads_on_rows(a):
        return jnp.concatenate([jnp.where(lane_k == h, a, 0.0) for h in range(GLA_HEADS)], axis=0).astype(BF16)

    def load(c):
        r = pl.ds(pl.multiple_of(c * lc, lc), lc)
        lg = lg_ref[0, r, :]
        hi = lg.astype(BF16)
        r1 = lg - hi.astype(F32)
        mid = r1.astype(BF16)
        lo = (r1 - mid.astype(F32)).astype(BF16)
        b = _dot(tri, hi) + _dot(tri, mid) + _dot(tri, lo)
        return dict(r=r, q=gq_ref[0, r, :], k=gk_ref[0, r, :], v=gv_ref[0, r, :].astype(F32), b=b)

    def local(d):
        q, k, v, b = d["q"], d["k"], d["v"], d["b"]
        v_b = v.astype(BF16)
        b_last = b[lc - 1:lc, :]
        d["decay"] = jnp.exp(b_last)
        d["q4"] = heads_on_rows(q * jnp.exp(b))
        kd = (k * jnp.exp(b_last - b)).astype(BF16)
        full = _tn(v_b, kd)
        upd = jnp.zeros((GLA_DV, GLA_K), F32)
        for h in range(GLA_HEADS):
            upd = upd + jnp.where(st_lane == h, full[h * GLA_DV:(h + 1) * GLA_DV], 0.0)
        d["upd"] = upd
        refs = [jnp.zeros((SUB, GLA_K), F32)]
        kst, vst = [], []
        for i_sub in range(1, n_sub):
            r_i = b[SUB * i_sub - 1:SUB * i_sub, :]
            refs.append(jnp.broadcast_to(r_i, (SUB, GLA_K)))
            kst.append(k[:SUB * i_sub] * jnp.exp(r_i - b[:SUB * i_sub]))
            vst.append(v[:SUB * i_sub])
        if stack_pad > n_stack:
            kst.append(jnp.zeros((stack_pad - n_stack, GLA_K), F32))
            vst.append(jnp.zeros((stack_pad - n_stack, GLA_V), F32))
        kst = jnp.concatenate(kst, axis=0).astype(BF16)
        vst = jnp.concatenate(vst, axis=0).astype(BF16)
        qp = q * jnp.exp(b - jnp.concatenate(refs, axis=0))
        a4 = _nt(heads_on_rows(qp), kst)
        d["o_off"] = [
            _dot(jnp.where(off_mask, a4[h * lc:(h + 1) * lc], 0.0).astype(BF16), vst[:, h * GLA_DV:(h + 1) * GLA_DV])
            for h in range(GLA_HEADS)]

    def same_sub(d):
        q, k, v, b = d["q"], d["k"], d["v"], d["b"]
        terms = []
        for i_sub in range(n_sub):
            rs = slice(SUB * i_sub, SUB * (i_sub + 1))
            q_i, k_i, b_i = q[rs], k[rs], b[rs]
            for j in range(SUB):
                arg = jnp.where(sub_row >= j, b_i - b_i[j:j + 1], NEG)
                terms.append(q_i * (k_i[j:j + 1] * jnp.exp(arg)))
        w = _dot(jnp.concatenate(terms, axis=0).astype(BF16), e_mat)
        o_diag = []
        for i_sub in range(n_sub):
            acc = None
            for j in range(SUB):
                row = SUB * i_sub + j
                t = w[row * SUB:(row + 1) * SUB] * v[row:row + 1]
                acc = t if acc is None else acc + t
            o_diag.append(acc)
        d["o_diag"] = jnp.concatenate(o_diag, axis=0)

    def carried(d):
        st = st_ref[...]
        d["o_heads"] = _nt(d["q4"], st.astype(BF16))
        st_ref[...] = d["decay"] * st + d["upd"]

    def finish(d):
        o = jnp.concatenate(
            [d["o_heads"][h * lc:(h + 1) * lc] + d["o_off"][h] for h in range(GLA_HEADS)], axis=1) + d["o_diag"]
        gon = gon_ref[...]
        normed = jnp.concatenate(
            [_rms(o[:, h * GLA_DV:(h + 1) * GLA_DV], gon) for h in range(GLA_HEADS)], axis=1)
        gg = gg_ref[0, d["r"], :]
        og_ref[0, d["r"], :] = (normed * (gg * jax.nn.sigmoid(gg))).astype(BF16)

    def body(it, carry):
        ds = [load(it * nch + u) for u in range(nch)]
        for stage in (local, same_sub, carried, finish):
            for d in ds:
                stage(d)
        return carry

    lax.fori_loop(0, n_chunks // nch, body, 0)

    @pl.when(pl.program_id(1) == pl.num_programs(1) - 1)
    def _():
        sfin_ref[0] = st_ref[...]


def _gla(gq, gk, lg, gv, gg, g_on, s0t, lc, tg):
    B, T, _ = gq.shape
    ng = T // tg
    tok = lambda w: pl.BlockSpec((1, tg, w), lambda b, g: (b, g, 0))
    st_spec = pl.BlockSpec((1, GLA_DV, GLA_K), lambda b, g: (b, 0, 0))
    S = jax.ShapeDtypeStruct
    return pl.pallas_call(
        functools.partial(_gla_kernel, lc, tg // lc, 2 if (tg // lc) % 2 == 0 else 1), grid=(B, ng),
        in_specs=[tok(GLA_K), tok(GLA_K), tok(GLA_K), tok(GLA_V), tok(GLA_V),
                  pl.BlockSpec((1, GLA_DV), lambda b, g: (0, 0)), st_spec],
        out_specs=[tok(GLA_V), st_spec],
        out_shape=[S((B, T, GLA_V), BF16), S((B, GLA_DV, GLA_K), F32)],
        scratch_shapes=[pltpu.VMEM((GLA_DV, GLA_K), F32)],
        compiler_params=pltpu.CompilerParams(dimension_semantics=("parallel", "arbitrary"),
                                             vmem_limit_bytes=VMEM_LIMIT),
        name="gla",
    )(gq, gk, lg, gv, gg, g_on, s0t)


def _ffn_kernel(final, x_ref, om_ref, og_ref, wo_ref, gf_ref, wg_ref, wu_ref, wd_ref, gfin_ref, y_ref):
    cat = jnp.concatenate([om_ref[0], og_ref[0]], axis=1)
    x1 = x_ref[0] + _dot(cat, wo_ref[...])
    hn = _rms(x1, gf_ref[...]).astype(BF16)
    a = _dot(hn, wg_ref[...])
    u = _dot(hn, wu_ref[...])
    act = (a * jax.nn.sigmoid(a) * u).astype(BF16)
    x2 = x1 + _dot(act, wd_ref[...])
    y_ref[0] = _rms(x2, gfin_ref[...]) if final else x2


def _ffn(x, o_mla, o_g, lw, g_final, tm, final):
    B, T, D = x.shape
    tok = lambda w: pl.BlockSpec((1, tm, w), lambda b, t: (b, t, 0))
    const = lambda a: pl.BlockSpec(a.shape, lambda b, t: (0,) * a.ndim, pipeline_mode=pl.Buffered(1))
    weights = (lw["w_o"], lw["g_ffn"], lw["w_g"], lw["w_u"], lw["w_down"], g_final)
    return pl.pallas_call(
        functools.partial(_ffn_kernel, final), grid=(B, T // tm),
        in_specs=[tok(D), tok(o_mla.shape[-1]), tok(o_g.shape[-1])] + [const(w) for w in weights],
        out_specs=tok(D), out_shape=jax.ShapeDtypeStruct((B, T, D), F32),
        compiler_params=pltpu.CompilerParams(dimension_semantics=("parallel", "parallel"),
                                             vmem_limit_bytes=VMEM_LIMIT),
        name="ffn_final" if final else "ffn",
    )(x, o_mla, o_g, *weights)


def _layer_weights(l, g_attn, w_in, g_qn, w_uq, g_kvn, w_ukv, w_a2, b_a2, g_gla_on, w_o, g_ffn, w_gu, w_down):
    d_model = w_in.shape[1]
    wi = w_in[l]
    c = np.cumsum([0, Q_LORA, KV_LORA, QK_ROPE, GLA_K, GLA_K, GLA_V, GLA_V, GATE_LR])
    sm = jnp.concatenate([wi[:, c[2]:c[3]], wi[:, c[7]:c[8]],
                          jnp.zeros((d_model, LANES - QK_ROPE - GATE_LR), F32)], axis=1)
    w_in_r = jnp.concatenate([wi[:, c[0]:c[2]], wi[:, c[3]:c[4]] * (GLA_DK ** -0.5), wi[:, c[4]:c[7]], sm], axis=1)
    wq = w_uq[l].reshape(Q_LORA, MLA_HEADS, QK_NOPE + QK_ROPE)
    w_qn_t = wq[:, :, :QK_NOPE].reshape(Q_LORA, -1).T
    w_qr_t = jnp.concatenate([wq[:, :, QK_NOPE:QK_NOPE + ROPE_HALF].reshape(Q_LORA, -1),
                              wq[:, :, QK_NOPE + ROPE_HALF:].reshape(Q_LORA, -1)], axis=1).T
    wkv = w_ukv[l].reshape(KV_LORA, MLA_HEADS, QK_NOPE + V_HEAD)
    wk_nope = jnp.pad(wkv[:, :, :QK_NOPE], ((0, 0), (0, 0), (0, HEAD_PAD - QK_NOPE))).reshape(KV_LORA, -1)
    place = np.zeros((LANES, MLA_HEADS, HEAD_PAD), np.float32)
    for d in range(QK_ROPE):
        place[d, :, QK_NOPE + d] = 1.0
    w_k = jnp.concatenate([wk_nope, jnp.asarray(place.reshape(LANES, -1))], axis=0)
    w_v_t = wkv[:, :, QK_NOPE:].reshape(KV_LORA, -1).T
    w_a2_p = jnp.zeros((LANES, GLA_K), F32).at[QK_ROPE:QK_ROPE + GATE_LR].set(w_a2[l])
    d_ff = w_down.shape[1]
    row = lambda v: v.reshape(1, -1)
    return {
        "g_attn": row(g_attn[l]), "w_in": w_in_r.astype(BF16),
        "g_qn": row(g_qn[l] * (ATTN_SCALE * LOG2E)), "w_qnT": w_qn_t.astype(BF16), "w_qrT": w_qr_t.astype(BF16),
        "g_kvn": row(g_kvn[l]), "w_k": w_k.astype(BF16), "w_vT": w_v_t.astype(BF16),
        "w_a2": w_a2_p.astype(BF16), "b_a2": row(b_a2[l]),
        "g_on": row(g_gla_on[l]), "w_o": w_o[l].astype(BF16), "g_ffn": row(g_ffn[l]),
        "w_g": w_gu[l][:, :d_ff].astype(BF16), "w_u": w_gu[l][:, d_ff:].astype(BF16),
        "w_down": w_down[l].astype(BF16),
    }


def _rope_tables(pos):
    inv = ROPE_THETA ** (-jnp.arange(ROPE_HALF, dtype=F32) / ROPE_HALF)
    ang = pos.astype(F32)[:, None] * inv[None, :]
    cos, sin = jnp.cos(ang), jnp.sin(ang)
    zeros = jnp.zeros((pos.shape[0], LANES - QK_ROPE), F32)
    cr = jnp.concatenate([cos, cos, zeros], axis=1)
    sr = jnp.concatenate([-sin, sin, zeros], axis=1)
    ct = jnp.tile(cos, (1, MLA_HEADS)).T
    st = jnp.tile(sin, (1, MLA_HEADS)).T
    return cr, sr, ct, st


def _pick(n, pref):
    for t in pref:
        if n % t == 0:
            return t
    return n


def kernel(x_prompt, x_sample, cache_ckv, cache_krope, state_gla, g_attn, w_in, g_qn, w_uq, g_kvn, w_ukv, w_a2, b_a2,
           g_gla_on, w_o, g_ffn, w_gu, w_down, g_final):
    depth = w_in.shape[0]
    B, T, D = x_prompt.shape
    nb, ts, _ = x_sample.shape
    past = cache_ckv.shape[2]
    assert T % CHUNK == 0 and ts % BF16_ROWS == 0 and ts <= CHUNK

    tabs_p = _rope_tables(jnp.arange(T, dtype=jnp.int32))
    tabs_s = _rope_tables(jnp.tile(past + jnp.arange(ts, dtype=jnp.int32), nb))
    tm_p = _pick(T, (512, 256, 128))
    tk = _pick(T, (512, 256, 128, 64))
    tq = 2 * tk if T % (2 * tk) == 0 else tk
    tg = _pick(T, (512, 256, 128, 64))
    g_fin = g_final.reshape(1, -1)

    xp = x_prompt
    xs = x_sample.reshape(1, nb * ts, D)
    outs = {k: [] for k in ("ckv_p", "kr_p", "st_p", "ckv_s", "kr_s", "st_s")}
    for l in range(depth):
        lw = _layer_weights(l, g_attn, w_in, g_qn, w_uq, g_kvn, w_ukv, w_a2, b_a2, g_gla_on, w_o, g_ffn, w_gu,
                            w_down)
        final = l == depth - 1
        ckv, kr, qt, k, vt, gq, gk, gv, gg, lg = _proj(xp, tabs_p, lw, tm_p, True)
        o_mla = _attn(qt, k, vt, tq, tk, tk, 0)
        s0 = jnp.zeros((B, GLA_DV, GLA_K), F32)
        o_g, st_fin = _gla(gq, gk, lg, gv, gg, lw["g_on"], s0, CHUNK, tg)
        xp = _ffn(xp, o_mla, o_g, lw, g_fin, tm_p, final)
        outs["ckv_p"].append(ckv), outs["kr_p"].append(kr)
        outs["st_p"].append(st_fin.reshape(B, GLA_DV, GLA_HEADS, GLA_DK).transpose(0, 2, 3, 1))
        ckv, kr, qt, gq, gk, gv, gg, lg = _proj(xs, tabs_s, lw, nb * ts, False)
        ckv, kr = ckv.reshape(nb, ts, -1), kr.reshape(nb, ts, -1)
        k, vt = _kvproj(cache_ckv[l], cache_krope[l], ckv, kr, lw)
        qt = qt.reshape(-1, nb, ts).transpose(1, 0, 2)
        o_mla = _attn(qt, k, vt, ts, past + ts, ts, past).reshape(1, nb * ts, -1)
        s0 = state_gla[l].astype(F32).transpose(0, 3, 1, 2).reshape(nb, GLA_DV, GLA_K)
        per_b = lambda a: a.reshape(nb, ts, -1)
        o_g, st_fin = _gla(per_b(gq), per_b(gk), per_b(lg), per_b(gv), per_b(gg), lw["g_on"], s0, ts, ts)
        xs = _ffn(xs, o_mla, o_g.reshape(1, nb * ts, -1), lw, g_fin, nb * ts, final)
        outs["ckv_s"].append(ckv), outs["kr_s"].append(kr)
        outs["st_s"].append(st_fin.reshape(nb, GLA_DV, GLA_HEADS, GLA_DK).transpose(0, 2, 3, 1))

    stack = lambda key: jnp.stack(outs[key])
    return (xp, xs.reshape(nb, ts, D), stack("ckv_p"), stack("kr_p"), stack("st_p"),
            stack("ckv_s"), stack("kr_s"), stack("st_s"))
```

```python
import functools

import jax
import jax.numpy as jnp
import numpy as np
from jax import lax
from jax.experimental import pallas as pl
from jax.experimental.pallas import tpu as pltpu

F32 = jnp.float32
BF16 = jnp.bfloat16

CHUNK = 64
EPS = 1e-6
MLA_HEADS = 8
QK_NOPE = 64
QK_ROPE = 32
ROPE_HALF = QK_ROPE // 2
V_HEAD = 64
Q_LORA = 256
KV_LORA = 256
ROPE_THETA = 10000.0
ATTN_SCALE = (QK_NOPE + QK_ROPE) ** -0.5
LOG2E = 1.4426950408889634
GLA_HEADS = 4
GLA_DK = 64
GLA_DV = 128
GATE_LR = 16
GATE_TAU = 16.0
GLA_K = GLA_HEADS * GLA_DK
GLA_V = GLA_HEADS * GLA_DV

LANES = 128
BF16_ROWS = 16
HEAD_PAD = 128
PV_ROWS = 256
SUB = 8
NEG = -1e30
EXP_RANGE = 64.0
NORM_MARGIN = 1.05
VMEM_LIMIT = 48 * 1024 * 1024

Z_CQ = 0
Z_CKV = Z_CQ + Q_LORA
Z_GQ = Z_CKV + KV_LORA
Z_GK = Z_GQ + GLA_K
Z_GV = Z_GK + GLA_K
Z_GG = Z_GV + GLA_V
Z_SM = Z_GG + GLA_V
Z_W = Z_SM + LANES


def _nt(a, b):
    return lax.dot_general(a, b, (((1,), (1,)), ((), ())), preferred_element_type=F32)


def _tn(a, b):
    return lax.dot_general(a, b, (((0,), (0,)), ((), ())), preferred_element_type=F32)


def _dot(a, b):
    return jnp.dot(a, b, preferred_element_type=F32)


def _rms(x, g):
    return x * lax.rsqrt(jnp.mean(x * x, axis=-1, keepdims=True) + EPS) * g


def _proj_kernel(emit_kv, x_ref, cr_ref, sr_ref, ct_ref, st_ref, gat_ref, win_ref, gqn_ref, wqn_ref, wqr_ref,
                 gkv_ref, wk_ref, wvt_ref, wa2_ref, ba2_ref, eq_ref, ek_ref, *outs):
    if emit_kv:
        ckv_ref, kr_ref, qt_ref, k_ref, vt_ref, qn2_ref, kn2_ref, gq_ref, gk_ref, gv_ref, gg_ref, lg_ref = outs
    else:
        ckv_ref, kr_ref, qt_ref, gq_ref, gk_ref, gv_ref, gg_ref, lg_ref = outs
    x = x_ref[0]
    h = _rms(x, gat_ref[...]).astype(BF16)
    z = _dot(h, win_ref[...])

    cqn = _rms(z[:, Z_CQ:Z_CQ + Q_LORA], gqn_ref[...]).astype(BF16)
    n_nope = MLA_HEADS * QK_NOPE
    n_rope = MLA_HEADS * ROPE_HALF
    q_nope = _nt(wqn_ref[...], cqn)
    qt_ref[0, 0:n_nope, :] = q_nope.astype(BF16)
    qr = _nt(wqr_ref[...], cqn)
    x1, x2 = qr[:n_rope], qr[n_rope:]
    ct, st = ct_ref[...], st_ref[...]
    q_r1, q_r2 = x1 * ct - x2 * st, x1 * st + x2 * ct
    qt_ref[0, n_nope:n_nope + n_rope, :] = q_r1.astype(BF16)
    qt_ref[0, n_nope + n_rope:, :] = q_r2.astype(BF16)

    ckv = _rms(z[:, Z_CKV:Z_CKV + KV_LORA], gkv_ref[...])
    ckv_ref[0] = ckv
    u = z[:, Z_SM:Z_SM + LANES]
    lane = lax.broadcasted_iota(jnp.int32, u.shape, 1)
    partner = jnp.where(lane < ROPE_HALF, pltpu.roll(u, LANES - ROPE_HALF, 1), pltpu.roll(u, ROPE_HALF, 1))
    kr = u * cr_ref[...] + partner * sr_ref[...]
    kr_ref[0] = kr[:, :QK_ROPE]
    if emit_kv:
        ckv_b = ckv.astype(BF16)
        k_all = _dot(jnp.concatenate([ckv_b, kr.astype(BF16)], axis=1), wk_ref[...])
        k_ref[0] = k_all.astype(BF16)
        vt_ref[0] = _nt(wvt_ref[...], ckv_b).astype(BF16)
        q_sq = jnp.concatenate([q_nope * q_nope, q_r1 * q_r1, q_r2 * q_r2], axis=0).astype(BF16)
        qn2 = jnp.max(_dot(eq_ref[...], q_sq), axis=1, keepdims=True)
        qn2_ref[0, 0] = jnp.broadcast_to(qn2, qn2_ref.shape[2:])
        kn2 = _dot((k_all * k_all).astype(BF16), ek_ref[...])
        kn2_ref[0, 0] = jnp.max(kn2, axis=0, keepdims=True)

    gq_ref[0] = z[:, Z_GQ:Z_GQ + GLA_K]
    gk_ref[0] = z[:, Z_GK:Z_GK + GLA_K]
    gv_ref[0] = z[:, Z_GV:Z_GV + GLA_V].astype(BF16)
    gg_ref[0] = z[:, Z_GG:Z_GG + GLA_V]
    xg = _dot(u.astype(BF16), wa2_ref[...]) + ba2_ref[...]
    lg_ref[0] = (jnp.minimum(xg, 0.0) - jnp.log1p(jnp.exp(-jnp.abs(xg)))) * (1.0 / GATE_TAU)


def _proj(x, tabs, lw, tm, emit_kv):
    B, T, D = x.shape
    nt = T // tm
    cr, sr, ct, st = tabs
    tok = lambda w: pl.BlockSpec((1, tm, w), lambda b, t: (b, t, 0))
    feat = lambda r: pl.BlockSpec((1, r, tm), lambda b, t: (b, 0, t))
    full = lambda a: pl.BlockSpec(a.shape, lambda b, t: (0,) * a.ndim)
    weights = (lw["g_attn"], lw["w_in"], lw["g_qn"], lw["w_qnT"], lw["w_qrT"], lw["g_kvn"], lw["w_k"], lw["w_vT"],
               lw["w_a2"], lw["b_a2"], lw["e_q"], lw["e_k"])
    in_specs = [tok(D),
                pl.BlockSpec((tm, LANES), lambda b, t: (t, 0)), pl.BlockSpec((tm, LANES), lambda b, t: (t, 0)),
                pl.BlockSpec((LANES, tm), lambda b, t: (0, t)), pl.BlockSpec((LANES, tm), lambda b, t: (0, t))]
    in_specs += [full(w) for w in weights]
    S = jax.ShapeDtypeStruct
    nq = MLA_HEADS * (QK_NOPE + QK_ROPE)
    out_shape = [S((B, T, KV_LORA), F32), S((B, T, QK_ROPE), F32), S((B, nq, T), BF16)]
    out_specs = [tok(KV_LORA), tok(QK_ROPE), feat(nq)]
    if emit_kv:
        out_shape += [S((B, T, MLA_HEADS * HEAD_PAD), BF16), S((B, MLA_HEADS * V_HEAD, T), BF16),
                      S((B, nt, BF16_ROWS, LANES), F32), S((B, nt, 1, LANES), F32)]
        out_specs += [tok(MLA_HEADS * HEAD_PAD), feat(MLA_HEADS * V_HEAD),
                      pl.BlockSpec((1, 1, BF16_ROWS, LANES), lambda b, t: (b, t, 0, 0)),
                      pl.BlockSpec((1, 1, 1, LANES), lambda b, t: (b, t, 0, 0))]
    out_shape += [S((B, T, GLA_K), F32), S((B, T, GLA_K), F32), S((B, T, GLA_V), BF16), S((B, T, GLA_V), F32),
                  S((B, T, GLA_K), F32)]
    out_specs += [tok(GLA_K), tok(GLA_K), tok(GLA_V), tok(GLA_V), tok(GLA_K)]
    return pl.pallas_call(
        functools.partial(_proj_kernel, emit_kv),
        grid=(B, nt), in_specs=in_specs, out_specs=out_specs, out_shape=out_shape,
        compiler_params=pltpu.CompilerParams(dimension_semantics=("parallel", "parallel"),
                                             vmem_limit_bytes=VMEM_LIMIT),
        name="proj_kv" if emit_kv else "proj",
    )(x, cr, sr, ct, st, *weights)


def _kvproj_kernel(cckv_ref, ckr_ref, nckv_ref, nkr_ref, wkc_ref, wkr_ref, wvt_ref, k_ref, vt_ref):
    ckv = jnp.concatenate([cckv_ref[0], nckv_ref[0]], axis=0).astype(BF16)
    kr = jnp.concatenate([ckr_ref[0], nkr_ref[0]], axis=0).astype(BF16)
    k_ref[0] = (_dot(ckv, wkc_ref[...]) + _dot(kr, wkr_ref[...])).astype(BF16)
    vt_ref[0] = _nt(wvt_ref[...], ckv).astype(BF16)


def _kvproj(cache_ckv, cache_kr, new_ckv, new_kr, lw):
    nb, past, _ = cache_ckv.shape
    new = new_ckv.shape[1]
    tot = past + new
    wkc, wkr = lw["w_k"][:KV_LORA], lw["w_k"][KV_LORA:KV_LORA + QK_ROPE]
    per_b = lambda a: pl.BlockSpec((1,) + a.shape[1:], lambda b: (b, 0, 0))
    full = lambda a: pl.BlockSpec(a.shape, lambda b: (0,) * a.ndim)
    S = jax.ShapeDtypeStruct
    return pl.pallas_call(
        _kvproj_kernel, grid=(nb,),
        in_specs=[per_b(cache_ckv), per_b(cache_kr), per_b(new_ckv), per_b(new_kr), full(wkc), full(wkr),
                  full(lw["w_vT"])],
        out_specs=[pl.BlockSpec((1, tot, MLA_HEADS * HEAD_PAD), lambda b: (b, 0, 0)),
                   pl.BlockSpec((1, MLA_HEADS * V_HEAD, tot), lambda b: (b, 0, 0))],
        out_shape=[S((nb, tot, MLA_HEADS * HEAD_PAD), BF16), S((nb, MLA_HEADS * V_HEAD, tot), BF16)],
        compiler_params=pltpu.CompilerParams(dimension_semantics=("parallel",), vmem_limit_bytes=VMEM_LIMIT),
        name="kvproj",
    )(cache_ckv, cache_kr, new_ckv, new_kr, wkc, wkr, lw["w_vT"])


def _attn_kernel(tq, tk, sw, q_off, norm_tiles, qi_ref, ki_ref, last_ref, qn2_ref, kn2_ref, qt_ref, k_ref, vt_ref,
                 o_ref, m_ref, l_ref, acc_ref, s_ref, p_ref, mmin_ref):
    nq_a, nk_a, nt_a = norm_tiles
    p_id = pl.program_id(1)
    qi, ki = qi_ref[p_id], ki_ref[p_id]
    n_nope = MLA_HEADS * QK_NOPE
    n_rope = MLA_HEADS * ROPE_HALF

    @pl.when(ki == 0)
    def _():
        m_ref[...] = jnp.full(m_ref.shape, NEG, F32)
        l_ref[...] = jnp.zeros(l_ref.shape, F32)
        acc_ref[...] = jnp.zeros(acc_ref.shape, F32)

    rb = PV_ROWS if tk % PV_ROWS == 0 else tk
    ones_rows = jnp.ones((BF16_ROWS, rb), BF16)

    n_strips = tq // sw

    def qk(h, i, masked):
        cols = slice(i * sw, (i + 1) * sw)
        qt_h = jnp.concatenate(
            [qt_ref[0, h * QK_NOPE:(h + 1) * QK_NOPE, cols],
             qt_ref[0, n_nope + h * ROPE_HALF:n_nope + (h + 1) * ROPE_HALF, cols],
             qt_ref[0, n_nope + n_rope + h * ROPE_HALF:n_nope + n_rope + (h + 1) * ROPE_HALF, cols],
             jnp.zeros((HEAD_PAD - QK_NOPE - QK_ROPE, sw), BF16)], axis=0)
        s = _dot(k_ref[0, :, h * HEAD_PAD:(h + 1) * HEAD_PAD], qt_h)
        if masked:
            kpos = ki * tk + lax.broadcasted_iota(jnp.int32, (tk, sw), 0)
            qpos = q_off + qi * tq + i * sw + lax.broadcasted_iota(jnp.int32, (tk, sw), 1)
            s = jnp.where((kpos // CHUNK) <= (qpos // CHUNK), s, NEG)
        return s

    def pv_rows(h, r):
        return jnp.concatenate([vt_ref[0, h * V_HEAD:(h + 1) * V_HEAD, r * rb:(r + 1) * rb], ones_rows], axis=0)

    def fast_step(modes):
        stages = [(h, i) for h in range(MLA_HEADS) for i in range(n_strips) if modes[i] != "skip"]

        def weights(t):
            h, i = stages[t]
            s = qk(h, i, modes[i] == "masked")
            p_ref[t % 2] = jnp.exp2(s - m_ref[h:h + 1, i * sw:(i + 1) * sw]).astype(BF16)

        weights(0)
        for t, (h, i) in enumerate(stages):
            if t + 1 < len(stages):
                weights(t + 1)
            pv = None
            for r in range(tk // rb):
                d = _dot(pv_rows(h, r), p_ref[t % 2, r * rb:(r + 1) * rb, :])
                pv = d if pv is None else pv + d
            rows, cols = slice(h * V_HEAD, (h + 1) * V_HEAD), slice(i * sw, (i + 1) * sw)
            acc_ref[rows, cols] = acc_ref[rows, cols] + pv[:V_HEAD]
            l_ref[h:h + 1, cols] = l_ref[h:h + 1, cols] + pv[V_HEAD:V_HEAD + 1]

    def step(modes):
        stages = [(h, i) for h in range(MLA_HEADS) for i in range(n_strips) if modes[i] != "skip"]

        def scores(t):
            h, i = stages[t]
            s = qk(h, i, modes[i] == "masked")
            s_ref[t % 2] = s
            return jnp.max(s, axis=0, keepdims=True)

        smax_next = scores(0)
        for t, (h, i) in enumerate(stages):
            smax = smax_next
            if t + 1 < len(stages):
                smax_next = scores(t + 1)
            cols = slice(i * sw, (i + 1) * sw)
            m_old = m_ref[h:h + 1, cols]
            m_new = jnp.maximum(m_old, smax)
            alpha = jnp.exp2(m_old - m_new)
            m_ref[h:h + 1, cols] = m_new
            rows = slice(h * V_HEAD, (h + 1) * V_HEAD)
            pv = None
            for r in range(tk // rb):
                p = jnp.exp2(s_ref[t % 2, r * rb:(r + 1) * rb, :] - m_new).astype(BF16)
                d = _dot(pv_rows(h, r), p)
                pv = d if pv is None else pv + d
            acc_ref[rows, cols] = alpha * acc_ref[rows, cols] + pv[:V_HEAD]
            l_ref[h:h + 1, cols] = alpha * l_ref[h:h + 1, cols] + pv[V_HEAD:V_HEAD + 1]
        for h in range(MLA_HEADS):
            mmin_ref[h] = jnp.min(m_ref[h:h + 1, :])

    if tk == sw and q_off % sw == 0:
        b = pl.program_id(0)
        fast = ki > 0
        for h in range(MLA_HEADS):
            q2 = qn2_ref[((b * nt_a + qi * nq_a) * MLA_HEADS) + h]
            for j in range(1, nq_a):
                q2 = jnp.maximum(q2, qn2_ref[((b * nt_a + qi * nq_a + j) * MLA_HEADS) + h])
            k2 = kn2_ref[((b * nt_a + ki * nk_a) * MLA_HEADS) + h]
            for j in range(1, nk_a):
                k2 = jnp.maximum(k2, kn2_ref[((b * nt_a + ki * nk_a + j) * MLA_HEADS) + h])
            room = mmin_ref[h] + EXP_RANGE
            fast = fast & (room > 0.0) & (q2 * k2 * NORM_MARGIN <= room * room)
        slow = jnp.logical_not(fast)
        rel = ki - (q_off // sw + qi * n_strips)
        pl.when(fast & (rel < 0))(lambda: fast_step(["full"] * n_strips))
        for r in range(n_strips):
            modes = ["skip"] * r + ["masked"] + ["full"] * (n_strips - r - 1)
            pl.when(fast & (rel == r))(functools.partial(fast_step, modes))
        pl.when(slow)(lambda: step(["masked"] * n_strips))
    else:
        step(["masked"] * n_strips)

    @pl.when(last_ref[p_id] == 1)
    def _():
        inv = 1.0 / l_ref[...]
        parts = [acc_ref[h * V_HEAD:(h + 1) * V_HEAD, :] * inv[h:h + 1, :] for h in range(MLA_HEADS)]
        o_ref[0] = jnp.concatenate(parts, axis=0).T.astype(BF16)


def _attn(qt, k, vt, qn2, kn2, tq, tk, sw, q_off):
    B, nqf, Tq = qt.shape
    Tk = k.shape[1]
    nq, nk = Tq // tq, Tk // tk
    nt_a = qn2.shape[1]
    norm_tiles = (max(1, nt_a // nq), max(1, nt_a // nk), nt_a)
    qi_l, ki_l, last_l = [], [], []
    for qi in range(nq):
        last_chunk = (q_off + qi * tq + tq - 1) // CHUNK
        kmax = min(nk - 1, ((last_chunk + 1) * CHUNK - 1) // tk)
        for ki in range(kmax + 1):
            qi_l.append(qi), ki_l.append(ki), last_l.append(int(ki == kmax))
    tabs = [jnp.asarray(np.asarray(a, np.int32)) for a in (qi_l, ki_l, last_l)]
    tabs += [qn2.reshape(-1), kn2.reshape(-1)]
    grid_spec = pltpu.PrefetchScalarGridSpec(
        num_scalar_prefetch=5, grid=(B, len(qi_l)),
        in_specs=[pl.BlockSpec((1, nqf, tq), lambda b, p, qi, ki, *_: (b, 0, qi[p])),
                  pl.BlockSpec((1, tk, MLA_HEADS * HEAD_PAD), lambda b, p, qi, ki, *_: (b, ki[p], 0)),
                  pl.BlockSpec((1, MLA_HEADS * V_HEAD, tk), lambda b, p, qi, ki, *_: (b, 0, ki[p]))],
        out_specs=pl.BlockSpec((1, tq, MLA_HEADS * V_HEAD), lambda b, p, qi, ki, *_: (b, qi[p], 0)),
        scratch_shapes=[pltpu.VMEM((MLA_HEADS, tq), F32), pltpu.VMEM((MLA_HEADS, tq), F32),
                        pltpu.VMEM((MLA_HEADS * V_HEAD, tq), F32), pltpu.VMEM((2, tk, sw), F32),
                        pltpu.VMEM((2, tk, sw), BF16), pltpu.SMEM((MLA_HEADS,), F32)])
    return pl.pallas_call(
        functools.partial(_attn_kernel, tq, tk, sw, q_off, norm_tiles), grid_spec=grid_spec,
        out_shape=jax.ShapeDtypeStruct((B, Tq, MLA_HEADS * V_HEAD), BF16),
        compiler_params=pltpu.CompilerParams(dimension_semantics=("parallel", "arbitrary"),
                                             vmem_limit_bytes=VMEM_LIMIT),
        name="attn",
    )(*tabs, qt, k, vt)


def _gla_kernel(lc, n_chunks, nch, gq_ref, gk_ref, lg_ref, gv_ref, gg_ref, gon_ref, s0_ref, og_ref, sfin_ref,
                st_ref):
    n_sub = lc // SUB
    n_stack = SUB * n_sub * (n_sub - 1) // 2
    stack_pad = -(-n_stack // LANES) * LANES

    @pl.when(pl.program_id(1) == 0)
    def _():
        st_ref[...] = s0_ref[0]

    lane_k = lax.broadcasted_iota(jnp.int32, (lc, GLA_K), 1) // GLA_DK
    ri = lax.broadcasted_iota(jnp.int32, (lc, lc), 0)
    ci = lax.broadcasted_iota(jnp.int32, (lc, lc), 1)
    tri = (ci <= ri).astype(BF16)
    e_r = lax.broadcasted_iota(jnp.int32, (GLA_K, GLA_V), 0) // GLA_DK
    e_c = lax.broadcasted_iota(jnp.int32, (GLA_K, GLA_V), 1) // GLA_DV
    e_mat = (e_r == e_c).astype(BF16)
    sub_row = lax.broadcasted_iota(jnp.int32, (SUB, GLA_K), 0)
    st_lane = lax.broadcasted_iota(jnp.int32, (GLA_DV, GLA_K), 1) // GLA_DK
    a_row = lax.broadcasted_iota(jnp.int32, (lc, stack_pad), 0) // SUB
    a_col = lax.broadcasted_iota(jnp.int32, (lc, stack_pad), 1)
    off_mask = jnp.zeros((lc, stack_pad), jnp.bool_)
    seg = 0
    for i_sub in range(1, n_sub):
        off_mask = off_mask | ((a_row == i_sub) & (a_col >= seg) & (a_col < seg + SUB * i_sub))
        seg += SUB * i_sub

    def heads_on_rows(a):
        return jnp.concatenate([jnp.where(lane_k == h, a, 0.0) for h in range(GLA_HEADS)], axis=0).astype(BF16)

    def load(c):
        r = pl.ds(pl.multiple_of(c * lc, lc), lc)
        lg = lg_ref[0, r, :]
        hi = lg.astype(BF16)
        r1 = lg - hi.astype(F32)
        mid = r1.astype(BF16)
        lo = (r1 - mid.astype(F32)).astype(BF16)
        b = _dot(tri, hi) + _dot(tri, mid) + _dot(tri, lo)
        return dict(r=r, q=gq_ref[0, r, :], k=gk_ref[0, r, :], v=gv_ref[0, r, :].astype(F32), b=b)

    def local(d):
        q, k, v, b = d["q"], d["k"], d["v"], d["b"]
        v_b = v.astype(BF16)
        b_last = b[lc - 1:lc, :]
        d["decay"] = jnp.exp(b_last)
        d["q4"] = heads_on_rows(q * jnp.exp(b))
        kd = (k * jnp.exp(b_last - b)).astype(BF16)
        full = _tn(v_b, kd)
        upd = jnp.zeros((GLA_DV, GLA_K), F32)
        for h in range(GLA_HEADS):
            upd = upd + jnp.where(st_lane == h, full[h * GLA_DV:(h + 1) * GLA_DV], 0.0)
        d["upd"] = upd
        refs = [jnp.zeros((SUB, GLA_K), F32)]
        kst, vst = [], []
        for i_sub in range(1, n_sub):
            r_i = b[SUB * i_sub - 1:SUB * i_sub, :]
            refs.append(jnp.broadcast_to(r_i, (SUB, GLA_K)))
            kst.append(k[:SUB * i_sub] * jnp.exp(r_i - b[:SUB * i_sub]))
            vst.append(v[:SUB * i_sub])
        if stack_pad > n_stack:
            kst.append(jnp.zeros((stack_pad - n_stack, GLA_K), F32))
            vst.append(jnp.zeros((stack_pad - n_stack, GLA_V), F32))
        kst = jnp.concatenate(kst, axis=0).astype(BF16)
        vst = jnp.concatenate(vst, axis=0).astype(BF16)
        qp = q * jnp.exp(b - jnp.concatenate(refs, axis=0))
        a4 = _nt(heads_on_rows(qp), kst)
        d["o_off"] = [
            _dot(jnp.where(off_mask, a4[h * lc:(h + 1) * lc], 0.0).astype(BF16), vst[:, h * GLA_DV:(h + 1) * GLA_DV])
            for h in range(GLA_HEADS)]

    def same_sub(d):
        q, k, v, b = d["q"], d["k"], d["v"], d["b"]
        terms = []
        for i_sub in range(n_sub):
            rs = slice(SUB * i_sub, SUB * (i_sub + 1))
            q_i, k_i, b_i = q[rs], k[rs], b[rs]
            for j in range(SUB):
                arg = jnp.where(sub_row >= j, b_i - b_i[j:j + 1], NEG)
                terms.append(q_i * (k_i[j:j + 1] * jnp.exp(arg)))
        w = _dot(jnp.concatenate(terms, axis=0).astype(BF16), e_mat)
        o_diag = []
        for i_sub in range(n_sub):
            acc = None
            for j in range(SUB):
                row = SUB * i_sub + j
                t = w[row * SUB:(row + 1) * SUB] * v[row:row + 1]
                acc = t if acc is None else acc + t
            o_diag.append(acc)
        d["o_diag"] = jnp.concatenate(o_diag, axis=0)

    def carried(d):
        st = st_ref[...]
        d["o_heads"] = _nt(d["q4"], st.astype(BF16))
        st_ref[...] = d["decay"] * st + d["upd"]

    def finish(d):
        o = jnp.concatenate(
            [d["o_heads"][h * lc:(h + 1) * lc] + d["o_off"][h] for h in range(GLA_HEADS)], axis=1) + d["o_diag"]
        gon = gon_ref[...]
        normed = jnp.concatenate(
            [_rms(o[:, h * GLA_DV:(h + 1) * GLA_DV], gon) for h in range(GLA_HEADS)], axis=1)
        gg = gg_ref[0, d["r"], :]
        og_ref[0, d["r"], :] = (normed * (gg * jax.nn.sigmoid(gg))).astype(BF16)

    def body(it, carry):
        ds = [load(it * nch + u) for u in range(nch)]
        for stage in (local, same_sub, carried, finish):
            for d in ds:
                stage(d)
        return carry

    lax.fori_loop(0, n_chunks // nch, body, 0)

    @pl.when(pl.program_id(1) == pl.num_programs(1) - 1)
    def _():
        sfin_ref[0] = st_ref[...]


def _gla(gq, gk, lg, gv, gg, g_on, s0t, lc, tg):
    B, T, _ = gq.shape
    ng = T // tg
    tok = lambda w: pl.BlockSpec((1, tg, w), lambda b, g: (b, g, 0))
    st_spec = pl.BlockSpec((1, GLA_DV, GLA_K), lambda b, g: (b, 0, 0))
    S = jax.ShapeDtypeStruct
    return pl.pallas_call(
        functools.partial(_gla_kernel, lc, tg // lc, _pick(tg // lc, (4, 2))), grid=(B, ng),
        in_specs=[tok(GLA_K), tok(GLA_K), tok(GLA_K), tok(GLA_V), tok(GLA_V),
                  pl.BlockSpec((1, GLA_DV), lambda b, g: (0, 0)), st_spec],
        out_specs=[tok(GLA_V), st_spec],
        out_shape=[S((B, T, GLA_V), BF16), S((B, GLA_DV, GLA_K), F32)],
        scratch_shapes=[pltpu.VMEM((GLA_DV, GLA_K), F32)],
        compiler_params=pltpu.CompilerParams(dimension_semantics=("parallel", "arbitrary"),
                                             vmem_limit_bytes=VMEM_LIMIT),
        name="gla",
    )(gq, gk, lg, gv, gg, g_on, s0t)


def _ffn_kernel(final, x_ref, om_ref, og_ref, wo_ref, gf_ref, wg_ref, wu_ref, wd_ref, gfin_ref, y_ref):
    cat = jnp.concatenate([om_ref[0], og_ref[0]], axis=1)
    x1 = x_ref[0] + _dot(cat, wo_ref[...])
    hn = _rms(x1, gf_ref[...]).astype(BF16)
    a = _dot(hn, wg_ref[...])
    u = _dot(hn, wu_ref[...])
    act = (a * jax.nn.sigmoid(a) * u).astype(BF16)
    x2 = x1 + _dot(act, wd_ref[...])
    y_ref[0] = _rms(x2, gfin_ref[...]) if final else x2


def _ffn(x, o_mla, o_g, lw, g_final, tm, final):
    B, T, D = x.shape
    tok = lambda w: pl.BlockSpec((1, tm, w), lambda b, t: (b, t, 0))
    const = lambda a: pl.BlockSpec(a.shape, lambda b, t: (0,) * a.ndim, pipeline_mode=pl.Buffered(1))
    weights = (lw["w_o"], lw["g_ffn"], lw["w_g"], lw["w_u"], lw["w_down"], g_final)
    return pl.pallas_call(
        functools.partial(_ffn_kernel, final), grid=(B, T // tm),
        in_specs=[tok(D), tok(o_mla.shape[-1]), tok(o_g.shape[-1])] + [const(w) for w in weights],
        out_specs=tok(D), out_shape=jax.ShapeDtypeStruct((B, T, D), F32),
        compiler_params=pltpu.CompilerParams(dimension_semantics=("parallel", "parallel"),
                                             vmem_limit_bytes=VMEM_LIMIT),
        name="ffn_final" if final else "ffn",
    )(x, o_mla, o_g, *weights)


def _layer_weights(l, g_attn, w_in, g_qn, w_uq, g_kvn, w_ukv, w_a2, b_a2, g_gla_on, w_o, g_ffn, w_gu, w_down):
    d_model = w_in.shape[1]
    wi = w_in[l]
    c = np.cumsum([0, Q_LORA, KV_LORA, QK_ROPE, GLA_K, GLA_K, GLA_V, GLA_V, GATE_LR])
    sm = jnp.concatenate([wi[:, c[2]:c[3]], wi[:, c[7]:c[8]],
                          jnp.zeros((d_model, LANES - QK_ROPE - GATE_LR), F32)], axis=1)
    w_in_r = jnp.concatenate([wi[:, c[0]:c[2]], wi[:, c[3]:c[4]] * (GLA_DK ** -0.5), wi[:, c[4]:c[7]], sm], axis=1)
    wq = w_uq[l].reshape(Q_LORA, MLA_HEADS, QK_NOPE + QK_ROPE)
    w_qn_t = wq[:, :, :QK_NOPE].reshape(Q_LORA, -1).T
    w_qr_t = jnp.concatenate([wq[:, :, QK_NOPE:QK_NOPE + ROPE_HALF].reshape(Q_LORA, -1),
                              wq[:, :, QK_NOPE + ROPE_HALF:].reshape(Q_LORA, -1)], axis=1).T
    wkv = w_ukv[l].reshape(KV_LORA, MLA_HEADS, QK_NOPE + V_HEAD)
    wk_nope = jnp.pad(wkv[:, :, :QK_NOPE], ((0, 0), (0, 0), (0, HEAD_PAD - QK_NOPE))).reshape(KV_LORA, -1)
    place = np.zeros((LANES, MLA_HEADS, HEAD_PAD), np.float32)
    for d in range(QK_ROPE):
        place[d, :, QK_NOPE + d] = 1.0
    w_k = jnp.concatenate([wk_nope, jnp.asarray(place.reshape(LANES, -1))], axis=0)
    w_v_t = wkv[:, :, QK_NOPE:].reshape(KV_LORA, -1).T
    w_a2_p = jnp.zeros((LANES, GLA_K), F32).at[QK_ROPE:QK_ROPE + GATE_LR].set(w_a2[l])
    d_ff = w_down.shape[1]
    row = lambda v: v.reshape(1, -1)
    n_nope, n_rope = MLA_HEADS * QK_NOPE, MLA_HEADS * ROPE_HALF
    e_q = np.zeros((BF16_ROWS, n_nope + 2 * n_rope), np.float32)
    e_k = np.zeros((MLA_HEADS * HEAD_PAD, LANES), np.float32)
    for h in range(MLA_HEADS):
        e_q[h, h * QK_NOPE:(h + 1) * QK_NOPE] = 1.0
        e_q[h, n_nope + h * ROPE_HALF:n_nope + (h + 1) * ROPE_HALF] = 1.0
        e_q[h, n_nope + n_rope + h * ROPE_HALF:n_nope + n_rope + (h + 1) * ROPE_HALF] = 1.0
        e_k[h * HEAD_PAD:(h + 1) * HEAD_PAD, h] = 1.0
    return {
        "e_q": jnp.asarray(e_q, BF16), "e_k": jnp.asarray(e_k, BF16),
        "g_attn": row(g_attn[l]), "w_in": w_in_r.astype(BF16),
        "g_qn": row(g_qn[l] * (ATTN_SCALE * LOG2E)), "w_qnT": w_qn_t.astype(BF16), "w_qrT": w_qr_t.astype(BF16),
        "g_kvn": row(g_kvn[l]), "w_k": w_k.astype(BF16), "w_vT": w_v_t.astype(BF16),
        "w_a2": w_a2_p.astype(BF16), "b_a2": row(b_a2[l]),
        "g_on": row(g_gla_on[l]), "w_o": w_o[l].astype(BF16), "g_ffn": row(g_ffn[l]),
        "w_g": w_gu[l][:, :d_ff].astype(BF16), "w_u": w_gu[l][:, d_ff:].astype(BF16),
        "w_down": w_down[l].astype(BF16),
    }


def _rope_tables(pos):
    inv = ROPE_THETA ** (-jnp.arange(ROPE_HALF, dtype=F32) / ROPE_HALF)
    ang = pos.astype(F32)[:, None] * inv[None, :]
    cos, sin = jnp.cos(ang), jnp.sin(ang)
    zeros = jnp.zeros((pos.shape[0], LANES - QK_ROPE), F32)
    cr = jnp.concatenate([cos, cos, zeros], axis=1)
    sr = jnp.concatenate([-sin, sin, zeros], axis=1)
    ct = jnp.tile(cos, (1, MLA_HEADS)).T
    st = jnp.tile(sin, (1, MLA_HEADS)).T
    return cr, sr, ct, st


def _pick(n, pref):
    for t in pref:
        if n % t == 0:
            return t
    return n


def kernel(x_prompt, x_sample, cache_ckv, cache_krope, state_gla, g_attn, w_in, g_qn, w_uq, g_kvn, w_ukv, w_a2, b_a2,
           g_gla_on, w_o, g_ffn, w_gu, w_down, g_final):
    depth = w_in.shape[0]
    B, T, D = x_prompt.shape
    nb, ts, _ = x_sample.shape
    past = cache_ckv.shape[2]
    assert T % CHUNK == 0 and ts % BF16_ROWS == 0 and ts <= CHUNK

    tabs_p = _rope_tables(jnp.arange(T, dtype=jnp.int32))
    tabs_s = _rope_tables(jnp.tile(past + jnp.arange(ts, dtype=jnp.int32), nb))
    tm_p = _pick(T, (512, 256, 128))
    tk = _pick(T, (512, 256, 128, 64))
    tq = 2 * tk if T % (2 * tk) == 0 else tk
    tg = _pick(T, (512, 256, 128, 64))
    g_fin = g_final.reshape(1, -1)

    xp = x_prompt
    xs = x_sample.reshape(1, nb * ts, D)
    outs = {k: [] for k in ("ckv_p", "kr_p", "st_p", "ckv_s", "kr_s", "st_s")}
    for l in range(depth):
        lw = _layer_weights(l, g_attn, w_in, g_qn, w_uq, g_kvn, w_ukv, w_a2, b_a2, g_gla_on, w_o, g_ffn, w_gu,
                            w_down)
        final = l == depth - 1
        ckv, kr, qt, k, vt, qn2, kn2, gq, gk, gv, gg, lg = _proj(xp, tabs_p, lw, tm_p, True)
        o_mla = _attn(qt, k, vt, qn2[:, :, :MLA_HEADS, 0], kn2[:, :, 0, :MLA_HEADS], tq, tk, tk, 0)
        s0 = jnp.zeros((B, GLA_DV, GLA_K), F32)
        o_g, st_fin = _gla(gq, gk, lg, gv, gg, lw["g_on"], s0, CHUNK, tg)
        xp = _ffn(xp, o_mla, o_g, lw, g_fin, tm_p, final)
        outs["ckv_p"].append(ckv), outs["kr_p"].append(kr)
        outs["st_p"].append(st_fin.reshape(B, GLA_DV, GLA_HEADS, GLA_DK).transpose(0, 2, 3, 1))
        ckv, kr, qt, gq, gk, gv, gg, lg = _proj(xs, tabs_s, lw, nb * ts, False)
        ckv, kr = ckv.reshape(nb, ts, -1), kr.reshape(nb, ts, -1)
        k, vt = _kvproj(cache_ckv[l], cache_krope[l], ckv, kr, lw)
        qt = qt.reshape(-1, nb, ts).transpose(1, 0, 2)
        no_norms = jnp.zeros((nb, 1, MLA_HEADS), F32)
        o_mla = _attn(qt, k, vt, no_norms, no_norms, ts, past + ts, ts, past).reshape(1, nb * ts, -1)
        s0 = state_gla[l].astype(F32).transpose(0, 3, 1, 2).reshape(nb, GLA_DV, GLA_K)
        per_b = lambda a: a.reshape(nb, ts, -1)
        o_g, st_fin = _gla(per_b(gq), per_b(gk), per_b(lg), per_b(gv), per_b(gg), lw["g_on"], s0, ts, ts)
        xs = _ffn(xs, o_mla, o_g.reshape(1, nb * ts, -1), lw, g_fin, nb * ts, final)
        outs["ckv_s"].append(ckv), outs["kr_s"].append(kr)
        outs["st_s"].append(st_fin.reshape(nb, GLA_DV, GLA_HEADS, GLA_DK).transpose(0, 2, 3, 1))

    stack = lambda key: jnp.stack(outs[key])
    return (xp, xs.reshape(nb, ts, D), stack("ckv_p"), stack("kr_p"), stack("st_p"),
            stack("ckv_s"), stack("kr_s"), stack("st_s"))
```

```python
import functools

import jax
import jax.numpy as jnp
import numpy as np
from jax import lax
from jax.experimental import pallas as pl
from jax.experimental.pallas import tpu as pltpu

F32 = jnp.float32
BF16 = jnp.bfloat16

CHUNK = 64
EPS = 1e-6
MLA_HEADS = 8
QK_NOPE = 64
QK_ROPE = 32
ROPE_HALF = QK_ROPE // 2
V_HEAD = 64
Q_LORA = 256
KV_LORA = 256
ROPE_THETA = 10000.0
ATTN_SCALE = (QK_NOPE + QK_ROPE) ** -0.5
LOG2E = 1.4426950408889634
GLA_HEADS = 4
GLA_DK = 64
GLA_DV = 128
GATE_LR = 16
GATE_TAU = 16.0
GLA_K = GLA_HEADS * GLA_DK
GLA_V = GLA_HEADS * GLA_DV

LANES = 128
BF16_ROWS = 16
HEAD_PAD = 128
PV_ROWS = 256
ATTN_TK = 512
ATTN_SW = 1024
ATTN_TQ = 2048
SUB = 8
NEG = -1e30
EXP_RANGE = 64.0
NORM_MARGIN = 1.05
VMEM_LIMIT = 48 * 1024 * 1024

Z_CQ = 0
Z_CKV = Z_CQ + Q_LORA
Z_GQ = Z_CKV + KV_LORA
Z_GK = Z_GQ + GLA_K
Z_GV = Z_GK + GLA_K
Z_GG = Z_GV + GLA_V
Z_SM = Z_GG + GLA_V
Z_W = Z_SM + LANES


def _nt(a, b):
    return lax.dot_general(a, b, (((1,), (1,)), ((), ())), preferred_element_type=F32)


def _tn(a, b):
    return lax.dot_general(a, b, (((0,), (0,)), ((), ())), preferred_element_type=F32)


def _dot(a, b):
    return jnp.dot(a, b, preferred_element_type=F32)


def _rms(x, g):
    return x * lax.rsqrt(jnp.mean(x * x, axis=-1, keepdims=True) + EPS) * g


def _proj_kernel(emit_kv, x_ref, cr_ref, sr_ref, ct_ref, st_ref, gat_ref, win_ref, gqn_ref, wqn_ref, wqr_ref,
                 gkv_ref, wk_ref, wvt_ref, wa2_ref, ba2_ref, eq_ref, ek_ref, *outs):
    if emit_kv:
        ckv_ref, kr_ref, qt_ref, k_ref, vt_ref, qn2_ref, kn2_ref, gq_ref, gk_ref, gv_ref, gg_ref, lg_ref = outs
    else:
        ckv_ref, kr_ref, qt_ref, gq_ref, gk_ref, gv_ref, gg_ref, lg_ref = outs
    x = x_ref[0]
    h = _rms(x, gat_ref[...]).astype(BF16)
    z = _dot(h, win_ref[...])

    cqn = _rms(z[:, Z_CQ:Z_CQ + Q_LORA], gqn_ref[...]).astype(BF16)
    n_nope = MLA_HEADS * QK_NOPE
    n_rope = MLA_HEADS * ROPE_HALF
    q_nope = _nt(wqn_ref[...], cqn)
    qt_ref[0, 0:n_nope, :] = q_nope.astype(BF16)
    qr = _nt(wqr_ref[...], cqn)
    x1, x2 = qr[:n_rope], qr[n_rope:]
    ct, st = ct_ref[...], st_ref[...]
    q_r1, q_r2 = x1 * ct - x2 * st, x1 * st + x2 * ct
    qt_ref[0, n_nope:n_nope + n_rope, :] = q_r1.astype(BF16)
    qt_ref[0, n_nope + n_rope:, :] = q_r2.astype(BF16)

    ckv = _rms(z[:, Z_CKV:Z_CKV + KV_LORA], gkv_ref[...])
    ckv_ref[0] = ckv
    u = z[:, Z_SM:Z_SM + LANES]
    lane = lax.broadcasted_iota(jnp.int32, u.shape, 1)
    partner = jnp.where(lane < ROPE_HALF, pltpu.roll(u, LANES - ROPE_HALF, 1), pltpu.roll(u, ROPE_HALF, 1))
    kr = u * cr_ref[...] + partner * sr_ref[...]
    kr_ref[0] = kr[:, :QK_ROPE]
    if emit_kv:
        ckv_b = ckv.astype(BF16)
        k_all = _dot(jnp.concatenate([ckv_b, kr.astype(BF16)], axis=1), wk_ref[...])
        k_ref[0] = k_all.astype(BF16)
        vt_ref[0] = _nt(wvt_ref[...], ckv_b).astype(BF16)
        q_sq = jnp.concatenate([q_nope * q_nope, q_r1 * q_r1, q_r2 * q_r2], axis=0).astype(BF16)
        qn2 = jnp.max(_dot(eq_ref[...], q_sq), axis=1, keepdims=True)
        qn2_ref[0, 0] = jnp.broadcast_to(qn2, qn2_ref.shape[2:])
        kn2 = _dot((k_all * k_all).astype(BF16), ek_ref[...])
        kn2_ref[0, 0] = jnp.max(kn2, axis=0, keepdims=True)

    gq_ref[0] = z[:, Z_GQ:Z_GQ + GLA_K]
    gk_ref[0] = z[:, Z_GK:Z_GK + GLA_K]
    gv_ref[0] = z[:, Z_GV:Z_GV + GLA_V].astype(BF16)
    gg_ref[0] = z[:, Z_GG:Z_GG + GLA_V]
    xg = _dot(u.astype(BF16), wa2_ref[...]) + ba2_ref[...]
    lg_ref[0] = (jnp.minimum(xg, 0.0) - jnp.log1p(jnp.exp(-jnp.abs(xg)))) * (1.0 / GATE_TAU)


def _proj(x, tabs, lw, tm, emit_kv):
    B, T, D = x.shape
    nt = T // tm
    cr, sr, ct, st = tabs
    tok = lambda w: pl.BlockSpec((1, tm, w), lambda b, t: (b, t, 0))
    feat = lambda r: pl.BlockSpec((1, r, tm), lambda b, t: (b, 0, t))
    full = lambda a: pl.BlockSpec(a.shape, lambda b, t: (0,) * a.ndim)
    weights = (lw["g_attn"], lw["w_in"], lw["g_qn"], lw["w_qnT"], lw["w_qrT"], lw["g_kvn"], lw["w_k"], lw["w_vT"],
               lw["w_a2"], lw["b_a2"], lw["e_q"], lw["e_k"])
    in_specs = [tok(D),
                pl.BlockSpec((tm, LANES), lambda b, t: (t, 0)), pl.BlockSpec((tm, LANES), lambda b, t: (t, 0)),
                pl.BlockSpec((LANES, tm), lambda b, t: (0, t)), pl.BlockSpec((LANES, tm), lambda b, t: (0, t))]
    in_specs += [full(w) for w in weights]
    S = jax.ShapeDtypeStruct
    nq = MLA_HEADS * (QK_NOPE + QK_ROPE)
    out_shape = [S((B, T, KV_LORA), F32), S((B, T, QK_ROPE), F32), S((B, nq, T), BF16)]
    out_specs = [tok(KV_LORA), tok(QK_ROPE), feat(nq)]
    if emit_kv:
        out_shape += [S((B, T, MLA_HEADS * HEAD_PAD), BF16), S((B, MLA_HEADS * V_HEAD, T), BF16),
                      S((B, nt, BF16_ROWS, LANES), F32), S((B, nt, 1, LANES), F32)]
        out_specs += [tok(MLA_HEADS * HEAD_PAD), feat(MLA_HEADS * V_HEAD),
                      pl.BlockSpec((1, 1, BF16_ROWS, LANES), lambda b, t: (b, t, 0, 0)),
                      pl.BlockSpec((1, 1, 1, LANES), lambda b, t: (b, t, 0, 0))]
    out_shape += [S((B, T, GLA_K), F32), S((B, T, GLA_K), F32), S((B, T, GLA_V), BF16), S((B, T, GLA_V), F32),
                  S((B, T, GLA_K), F32)]
    out_specs += [tok(GLA_K), tok(GLA_K), tok(GLA_V), tok(GLA_V), tok(GLA_K)]
    return pl.pallas_call(
        functools.partial(_proj_kernel, emit_kv),
        grid=(B, nt), in_specs=in_specs, out_specs=out_specs, out_shape=out_shape,
        compiler_params=pltpu.CompilerParams(dimension_semantics=("parallel", "parallel"),
                                             vmem_limit_bytes=VMEM_LIMIT),
        name="proj_kv" if emit_kv else "proj",
    )(x, cr, sr, ct, st, *weights)


def _kvproj_kernel(cckv_ref, ckr_ref, nckv_ref, nkr_ref, wkc_ref, wkr_ref, wvt_ref, k_ref, vt_ref):
    ckv = jnp.concatenate([cckv_ref[0], nckv_ref[0]], axis=0).astype(BF16)
    kr = jnp.concatenate([ckr_ref[0], nkr_ref[0]], axis=0).astype(BF16)
    k_ref[0] = (_dot(ckv, wkc_ref[...]) + _dot(kr, wkr_ref[...])).astype(BF16)
    vt_ref[0] = _nt(wvt_ref[...], ckv).astype(BF16)


def _kvproj(cache_ckv, cache_kr, new_ckv, new_kr, lw):
    nb, past, _ = cache_ckv.shape
    new = new_ckv.shape[1]
    tot = past + new
    wkc, wkr = lw["w_k"][:KV_LORA], lw["w_k"][KV_LORA:KV_LORA + QK_ROPE]
    per_b = lambda a: pl.BlockSpec((1,) + a.shape[1:], lambda b: (b, 0, 0))
    full = lambda a: pl.BlockSpec(a.shape, lambda b: (0,) * a.ndim)
    S = jax.ShapeDtypeStruct
    return pl.pallas_call(
        _kvproj_kernel, grid=(nb,),
        in_specs=[per_b(cache_ckv), per_b(cache_kr), per_b(new_ckv), per_b(new_kr), full(wkc), full(wkr),
                  full(lw["w_vT"])],
        out_specs=[pl.BlockSpec((1, tot, MLA_HEADS * HEAD_PAD), lambda b: (b, 0, 0)),
                   pl.BlockSpec((1, MLA_HEADS * V_HEAD, tot), lambda b: (b, 0, 0))],
        out_shape=[S((nb, tot, MLA_HEADS * HEAD_PAD), BF16), S((nb, MLA_HEADS * V_HEAD, tot), BF16)],
        compiler_params=pltpu.CompilerParams(dimension_semantics=("parallel",), vmem_limit_bytes=VMEM_LIMIT),
        name="kvproj",
    )(cache_ckv, cache_kr, new_ckv, new_kr, wkc, wkr, lw["w_vT"])


def _attn_kernel(tq, tk, sw, q_off, norm_tiles, qi_ref, ki_ref, last_ref, qn2_ref, kn2_ref, qt_ref, k_ref, vt_ref,
                 o_ref, m_ref, l_ref, acc_ref, s_ref, p_ref, mmin_ref):
    nq_a, nk_a, nt_a = norm_tiles
    p_id = pl.program_id(1)
    qi, ki = qi_ref[p_id], ki_ref[p_id]
    n_nope = MLA_HEADS * QK_NOPE
    n_rope = MLA_HEADS * ROPE_HALF

    @pl.when(ki == 0)
    def _():
        m_ref[...] = jnp.full(m_ref.shape, NEG, F32)
        l_ref[...] = jnp.zeros(l_ref.shape, F32)
        acc_ref[...] = jnp.zeros(acc_ref.shape, F32)

    rb = PV_ROWS if tk % PV_ROWS == 0 else tk
    ones_rows = jnp.ones((BF16_ROWS, rb), BF16)

    n_strips = tq // sw

    def qk(h, i, masked):
        cols = slice(i * sw, (i + 1) * sw)
        qt_h = jnp.concatenate(
            [qt_ref[0, h * QK_NOPE:(h + 1) * QK_NOPE, cols],
             qt_ref[0, n_nope + h * ROPE_HALF:n_nope + (h + 1) * ROPE_HALF, cols],
             qt_ref[0, n_nope + n_rope + h * ROPE_HALF:n_nope + n_rope + (h + 1) * ROPE_HALF, cols],
             jnp.zeros((HEAD_PAD - QK_NOPE - QK_ROPE, sw), BF16)], axis=0)
        s = _dot(k_ref[0, :, h * HEAD_PAD:(h + 1) * HEAD_PAD], qt_h)
        if masked:
            kpos = ki * tk + lax.broadcasted_iota(jnp.int32, (tk, sw), 0)
            qpos = q_off + qi * tq + i * sw + lax.broadcasted_iota(jnp.int32, (tk, sw), 1)
            s = jnp.where((kpos // CHUNK) <= (qpos // CHUNK), s, NEG)
        return s

    def pv_rows(h, r):
        return jnp.concatenate([vt_ref[0, h * V_HEAD:(h + 1) * V_HEAD, r * rb:(r + 1) * rb], ones_rows], axis=0)

    def fast_step(modes):
        stages = [(h, i) for h in range(MLA_HEADS) for i in range(n_strips) if modes[i] != "skip"]

        def weights(t):
            h, i = stages[t]
            s = qk(h, i, modes[i] == "masked")
            p_ref[t % 2] = jnp.exp2(s - m_ref[h:h + 1, i * sw:(i + 1) * sw]).astype(BF16)

        weights(0)
        for t, (h, i) in enumerate(stages):
            if t + 1 < len(stages):
                weights(t + 1)
            pv = None
            for r in range(tk // rb):
                d = _dot(pv_rows(h, r), p_ref[t % 2, r * rb:(r + 1) * rb, :])
                pv = d if pv is None else pv + d
            rows, cols = slice(h * V_HEAD, (h + 1) * V_HEAD), slice(i * sw, (i + 1) * sw)
            acc_ref[rows, cols] = acc_ref[rows, cols] + pv[:V_HEAD]
            l_ref[h:h + 1, cols] = l_ref[h:h + 1, cols] + pv[V_HEAD:V_HEAD + 1]

    def step(modes):
        stages = [(h, i) for h in range(MLA_HEADS) for i in range(n_strips) if modes[i] != "skip"]

        def scores(t):
            h, i = stages[t]
            s = qk(h, i, modes[i] == "masked")
            s_ref[t % 2] = s
            return jnp.max(s, axis=0, keepdims=True)

        smax_next = scores(0)
        for t, (h, i) in enumerate(stages):
            smax = smax_next
            if t + 1 < len(stages):
                smax_next = scores(t + 1)
            cols = slice(i * sw, (i + 1) * sw)
            m_old = m_ref[h:h + 1, cols]
            m_new = jnp.maximum(m_old, smax)
            alpha = jnp.exp2(m_old - m_new)
            m_ref[h:h + 1, cols] = m_new
            rows = slice(h * V_HEAD, (h + 1) * V_HEAD)
            pv = None
            for r in range(tk // rb):
                p = jnp.exp2(s_ref[t % 2, r * rb:(r + 1) * rb, :] - m_new).astype(BF16)
                d = _dot(pv_rows(h, r), p)
                pv = d if pv is None else pv + d
            acc_ref[rows, cols] = alpha * acc_ref[rows, cols] + pv[:V_HEAD]
            l_ref[h:h + 1, cols] = alpha * l_ref[h:h + 1, cols] + pv[V_HEAD:V_HEAD + 1]
        for h in range(MLA_HEADS):
            mmin_ref[h] = jnp.min(m_ref[h:h + 1, :])

    if sw % tk == 0 and q_off % sw == 0:
        b = pl.program_id(0)
        fast = ki > 0
        for h in range(MLA_HEADS):
            q2 = qn2_ref[((b * nt_a + qi * nq_a) * MLA_HEADS) + h]
            for j in range(1, nq_a):
                q2 = jnp.maximum(q2, qn2_ref[((b * nt_a + qi * nq_a + j) * MLA_HEADS) + h])
            k2 = kn2_ref[((b * nt_a + ki * nk_a) * MLA_HEADS) + h]
            for j in range(1, nk_a):
                k2 = jnp.maximum(k2, kn2_ref[((b * nt_a + ki * nk_a + j) * MLA_HEADS) + h])
            room = mmin_ref[h] + EXP_RANGE
            fast = fast & (room > 0.0) & (q2 * k2 * NORM_MARGIN <= room * room)
        slow = jnp.logical_not(fast)
        per = sw // tk
        rel = ki - (q_off // tk + qi * n_strips * per)
        pl.when(fast & (rel < 0))(lambda: fast_step(["full"] * n_strips))
        for r in range(n_strips * per):
            modes = ["full" if r < i * per else "masked" if r < (i + 1) * per else "skip" for i in range(n_strips)]
            pl.when(fast & (rel == r))(functools.partial(fast_step, modes))
        pl.when(slow)(lambda: step(["masked"] * n_strips))
    else:
        step(["masked"] * n_strips)

    @pl.when(last_ref[p_id] == 1)
    def _():
        inv = 1.0 / l_ref[...]
        parts = [acc_ref[h * V_HEAD:(h + 1) * V_HEAD, :] * inv[h:h + 1, :] for h in range(MLA_HEADS)]
        o_ref[0] = jnp.concatenate(parts, axis=0).T.astype(BF16)


def _attn(qt, k, vt, qn2, kn2, tq, tk, sw, q_off):
    B, nqf, Tq = qt.shape
    Tk = k.shape[1]
    nq, nk = Tq // tq, Tk // tk
    nt_a = qn2.shape[1]
    norm_tiles = (max(1, nt_a // nq), max(1, nt_a // nk), nt_a)
    qi_l, ki_l, last_l = [], [], []
    for qi in range(nq):
        last_chunk = (q_off + qi * tq + tq - 1) // CHUNK
        kmax = min(nk - 1, ((last_chunk + 1) * CHUNK - 1) // tk)
        for ki in range(kmax + 1):
            qi_l.append(qi), ki_l.append(ki), last_l.append(int(ki == kmax))
    tabs = [jnp.asarray(np.asarray(a, np.int32)) for a in (qi_l, ki_l, last_l)]
    tabs += [qn2.reshape(-1), kn2.reshape(-1)]
    grid_spec = pltpu.PrefetchScalarGridSpec(
        num_scalar_prefetch=5, grid=(B, len(qi_l)),
        in_specs=[pl.BlockSpec((1, nqf, tq), lambda b, p, qi, ki, *_: (b, 0, qi[p])),
                  pl.BlockSpec((1, tk, MLA_HEADS * HEAD_PAD), lambda b, p, qi, ki, *_: (b, ki[p], 0)),
                  pl.BlockSpec((1, MLA_HEADS * V_HEAD, tk), lambda b, p, qi, ki, *_: (b, 0, ki[p]))],
        out_specs=pl.BlockSpec((1, tq, MLA_HEADS * V_HEAD), lambda b, p, qi, ki, *_: (b, qi[p], 0)),
        scratch_shapes=[pltpu.VMEM((MLA_HEADS, tq), F32), pltpu.VMEM((MLA_HEADS, tq), F32),
                        pltpu.VMEM((MLA_HEADS * V_HEAD, tq), F32), pltpu.VMEM((2, tk, sw), F32),
                        pltpu.VMEM((2, tk, sw), BF16), pltpu.SMEM((MLA_HEADS,), F32)])
    return pl.pallas_call(
        functools.partial(_attn_kernel, tq, tk, sw, q_off, norm_tiles), grid_spec=grid_spec,
        out_shape=jax.ShapeDtypeStruct((B, Tq, MLA_HEADS * V_HEAD), BF16),
        compiler_params=pltpu.CompilerParams(dimension_semantics=("parallel", "arbitrary"),
                                             vmem_limit_bytes=VMEM_LIMIT),
        name="attn",
    )(*tabs, qt, k, vt)


def _gla_kernel(lc, n_chunks, nch, gq_ref, gk_ref, lg_ref, gv_ref, gg_ref, gon_ref, s0_ref, og_ref, sfin_ref,
                st_ref):
    n_sub = lc // SUB
    n_stack = SUB * n_sub * (n_sub - 1) // 2
    stack_pad = -(-n_stack // LANES) * LANES

    @pl.when(pl.program_id(1) == 0)
    def _():
        st_ref[...] = s0_ref[0]

    lane_k = lax.broadcasted_iota(jnp.int32, (lc, GLA_K), 1) // GLA_DK
    ri = lax.broadcasted_iota(jnp.int32, (lc, lc), 0)
    ci = lax.broadcasted_iota(jnp.int32, (lc, lc), 1)
    tri = (ci <= ri).astype(BF16)
    e_r = lax.broadcasted_iota(jnp.int32, (GLA_K, GLA_V), 0) // GLA_DK
    e_c = lax.broadcasted_iota(jnp.int32, (GLA_K, GLA_V), 1) // GLA_DV
    e_mat = (e_r == e_c).astype(BF16)
    sub_row = lax.broadcasted_iota(jnp.int32, (SUB, GLA_K), 0)
    st_lane = lax.broadcasted_iota(jnp.int32, (GLA_DV, GLA_K), 1) // GLA_DK
    a_row = lax.broadcasted_iota(jnp.int32, (lc, stack_pad), 0) // SUB
    a_col = lax.broadcasted_iota(jnp.int32, (lc, stack_pad), 1)
    off_mask = jnp.zeros((lc, stack_pad), jnp.bool_)
    seg = 0
    for i_sub in range(1, n_sub):
        off_mask = off_mask | ((a_row == i_sub) & (a_col >= seg) & (a_col < seg + SUB * i_sub))
        seg += SUB * i_sub

    def heads_on_rows(a):
        return jnp.concatenate([jnp.where(lane_k == h, a, 0.0) for h in range(GLA_HEADS)], axis=0).astype(BF16)

    def load(c):
        r = pl.ds(pl.multiple_of(c * lc, lc), lc)
        lg = lg_ref[0, r, :]
        hi = lg.astype(BF16)
        r1 = lg - hi.astype(F32)
        mid = r1.astype(BF16)
        lo = (r1 - mid.astype(F32)).astype(BF16)
        b = _dot(tri, hi) + _dot(tri, mid) + _dot(tri, lo)
        return dict(r=r, q=gq_ref[0, r, :], k=gk_ref[0, r, :], v=gv_ref[0, r, :].astype(F32), b=b)

    def local(d):
        q, k, v, b = d["q"], d["k"], d["v"], d["b"]
        v_b = v.astype(BF16)
        b_last = b[lc - 1:lc, :]
        d["decay"] = jnp.exp(b_last)
        d["q4"] = heads_on_rows(q * jnp.exp(b))
        kd = (k * jnp.exp(b_last - b)).astype(BF16)
        full = _tn(v_b, kd)
        upd = jnp.zeros((GLA_DV, GLA_K), F32)
        for h in range(GLA_HEADS):
            upd = upd + jnp.where(st_lane == h, full[h * GLA_DV:(h + 1) * GLA_DV], 0.0)
        d["upd"] = upd
        refs = [jnp.zeros((SUB, GLA_K), F32)]
        kst, vst = [], []
        for i_sub in range(1, n_sub):
            r_i = b[SUB * i_sub - 1:SUB * i_sub, :]
            refs.append(jnp.broadcast_to(r_i, (SUB, GLA_K)))
            kst.append(k[:SUB * i_sub] * jnp.exp(r_i - b[:SUB * i_sub]))
            vst.append(v[:SUB * i_sub])
        if stack_pad > n_stack:
            kst.append(jnp.zeros((stack_pad - n_stack, GLA_K), F32))
            vst.append(jnp.zeros((stack_pad - n_stack, GLA_V), F32))
        kst = jnp.concatenate(kst, axis=0).astype(BF16)
        vst = jnp.concatenate(vst, axis=0).astype(BF16)
        qp = q * jnp.exp(b - jnp.concatenate(refs, axis=0))
        a4 = _nt(heads_on_rows(qp), kst)
        d["o_off"] = [
            _dot(jnp.where(off_mask, a4[h * lc:(h + 1) * lc], 0.0).astype(BF16), vst[:, h * GLA_DV:(h + 1) * GLA_DV])
            for h in range(GLA_HEADS)]

    def same_sub(d):
        q, k, v, b = d["q"], d["k"], d["v"], d["b"]
        terms = []
        for i_sub in range(n_sub):
            rs = slice(SUB * i_sub, SUB * (i_sub + 1))
            q_i, k_i, b_i = q[rs], k[rs], b[rs]
            for j in range(SUB):
                arg = jnp.where(sub_row >= j, b_i - b_i[j:j + 1], NEG)
                terms.append(q_i * (k_i[j:j + 1] * jnp.exp(arg)))
        w = _dot(jnp.concatenate(terms, axis=0).astype(BF16), e_mat)
        o_diag = []
        for i_sub in range(n_sub):
            acc = None
            for j in range(SUB):
                row = SUB * i_sub + j
                t = w[row * SUB:(row + 1) * SUB] * v[row:row + 1]
                acc = t if acc is None else acc + t
            o_diag.append(acc)
        d["o_diag"] = jnp.concatenate(o_diag, axis=0)

    def carried(d):
        st = st_ref[...]
        d["o_heads"] = _nt(d["q4"], st.astype(BF16))
        st_ref[...] = d["decay"] * st + d["upd"]

    def finish(d):
        o = jnp.concatenate(
            [d["o_heads"][h * lc:(h + 1) * lc] + d["o_off"][h] for h in range(GLA_HEADS)], axis=1) + d["o_diag"]
        gon = gon_ref[...]
        normed = jnp.concatenate(
            [_rms(o[:, h * GLA_DV:(h + 1) * GLA_DV], gon) for h in range(GLA_HEADS)], axis=1)
        gg = gg_ref[0, d["r"], :]
        og_ref[0, d["r"], :] = (normed * (gg * jax.nn.sigmoid(gg))).astype(BF16)

    def body(it, carry):
        ds = [load(it * nch + u) for u in range(nch)]
        for stage in (local, same_sub, carried, finish):
            for d in ds:
                stage(d)
        return carry

    lax.fori_loop(0, n_chunks // nch, body, 0)

    @pl.when(pl.program_id(1) == pl.num_programs(1) - 1)
    def _():
        sfin_ref[0] = st_ref[...]


def _gla(gq, gk, lg, gv, gg, g_on, s0t, lc, tg):
    B, T, _ = gq.shape
    ng = T // tg
    tok = lambda w: pl.BlockSpec((1, tg, w), lambda b, g: (b, g, 0))
    st_spec = pl.BlockSpec((1, GLA_DV, GLA_K), lambda b, g: (b, 0, 0))
    S = jax.ShapeDtypeStruct
    return pl.pallas_call(
        functools.partial(_gla_kernel, lc, tg // lc, _pick(tg // lc, (4, 2))), grid=(B, ng),
        in_specs=[tok(GLA_K), tok(GLA_K), tok(GLA_K), tok(GLA_V), tok(GLA_V),
                  pl.BlockSpec((1, GLA_DV), lambda b, g: (0, 0)), st_spec],
        out_specs=[tok(GLA_V), st_spec],
        out_shape=[S((B, T, GLA_V), BF16), S((B, GLA_DV, GLA_K), F32)],
        scratch_shapes=[pltpu.VMEM((GLA_DV, GLA_K), F32)],
        compiler_params=pltpu.CompilerParams(dimension_semantics=("parallel", "arbitrary"),
                                             vmem_limit_bytes=VMEM_LIMIT),
        name="gla",
    )(gq, gk, lg, gv, gg, g_on, s0t)


def _ffn_kernel(final, x_ref, om_ref, og_ref, wo_ref, gf_ref, wg_ref, wu_ref, wd_ref, gfin_ref, y_ref):
    cat = jnp.concatenate([om_ref[0], og_ref[0]], axis=1)
    x1 = x_ref[0] + _dot(cat, wo_ref[...])
    hn = _rms(x1, gf_ref[...]).astype(BF16)
    a = _dot(hn, wg_ref[...])
    u = _dot(hn, wu_ref[...])
    act = (a * jax.nn.sigmoid(a) * u).astype(BF16)
    x2 = x1 + _dot(act, wd_ref[...])
    y_ref[0] = _rms(x2, gfin_ref[...]) if final else x2


def _ffn(x, o_mla, o_g, lw, g_final, tm, final):
    B, T, D = x.shape
    tok = lambda w: pl.BlockSpec((1, tm, w), lambda b, t: (b, t, 0))
    const = lambda a: pl.BlockSpec(a.shape, lambda b, t: (0,) * a.ndim, pipeline_mode=pl.Buffered(1))
    weights = (lw["w_o"], lw["g_ffn"], lw["w_g"], lw["w_u"], lw["w_down"], g_final)
    return pl.pallas_call(
        functools.partial(_ffn_kernel, final), grid=(B, T // tm),
        in_specs=[tok(D), tok(o_mla.shape[-1]), tok(o_g.shape[-1])] + [const(w) for w in weights],
        out_specs=tok(D), out_shape=jax.ShapeDtypeStruct((B, T, D), F32),
        compiler_params=pltpu.CompilerParams(dimension_semantics=("parallel", "parallel"),
                                             vmem_limit_bytes=VMEM_LIMIT),
        name="ffn_final" if final else "ffn",
    )(x, o_mla, o_g, *weights)


def _layer_weights(l, g_attn, w_in, g_qn, w_uq, g_kvn, w_ukv, w_a2, b_a2, g_gla_on, w_o, g_ffn, w_gu, w_down):
    d_model = w_in.shape[1]
    wi = w_in[l]
    c = np.cumsum([0, Q_LORA, KV_LORA, QK_ROPE, GLA_K, GLA_K, GLA_V, GLA_V, GATE_LR])
    sm = jnp.concatenate([wi[:, c[2]:c[3]], wi[:, c[7]:c[8]],
                          jnp.zeros((d_model, LANES - QK_ROPE - GATE_LR), F32)], axis=1)
    w_in_r = jnp.concatenate([wi[:, c[0]:c[2]], wi[:, c[3]:c[4]] * (GLA_DK ** -0.5), wi[:, c[4]:c[7]], sm], axis=1)
    wq = w_uq[l].reshape(Q_LORA, MLA_HEADS, QK_NOPE + QK_ROPE)
    w_qn_t = wq[:, :, :QK_NOPE].reshape(Q_LORA, -1).T
    w_qr_t = jnp.concatenate([wq[:, :, QK_NOPE:QK_NOPE + ROPE_HALF].reshape(Q_LORA, -1),
                              wq[:, :, QK_NOPE + ROPE_HALF:].reshape(Q_LORA, -1)], axis=1).T
    wkv = w_ukv[l].reshape(KV_LORA, MLA_HEADS, QK_NOPE + V_HEAD)
    wk_nope = jnp.pad(wkv[:, :, :QK_NOPE], ((0, 0), (0, 0), (0, HEAD_PAD - QK_NOPE))).reshape(KV_LORA, -1)
    place = np.zeros((LANES, MLA_HEADS, HEAD_PAD), np.float32)
    for d in range(QK_ROPE):
        place[d, :, QK_NOPE + d] = 1.0
    w_k = jnp.concatenate([wk_nope, jnp.asarray(place.reshape(LANES, -1))], axis=0)
    w_v_t = wkv[:, :, QK_NOPE:].reshape(KV_LORA, -1).T
    w_a2_p = jnp.zeros((LANES, GLA_K), F32).at[QK_ROPE:QK_ROPE + GATE_LR].set(w_a2[l])
    d_ff = w_down.shape[1]
    row = lambda v: v.reshape(1, -1)
    n_nope, n_rope = MLA_HEADS * QK_NOPE, MLA_HEADS * ROPE_HALF
    e_q = np.zeros((BF16_ROWS, n_nope + 2 * n_rope), np.float32)
    e_k = np.zeros((MLA_HEADS * HEAD_PAD, LANES), np.float32)
    for h in range(MLA_HEADS):
        e_q[h, h * QK_NOPE:(h + 1) * QK_NOPE] = 1.0
        e_q[h, n_nope + h * ROPE_HALF:n_nope + (h + 1) * ROPE_HALF] = 1.0
        e_q[h, n_nope + n_rope + h * ROPE_HALF:n_nope + n_rope + (h + 1) * ROPE_HALF] = 1.0
        e_k[h * HEAD_PAD:(h + 1) * HEAD_PAD, h] = 1.0
    return {
        "e_q": jnp.asarray(e_q, BF16), "e_k": jnp.asarray(e_k, BF16),
        "g_attn": row(g_attn[l]), "w_in": w_in_r.astype(BF16),
        "g_qn": row(g_qn[l] * (ATTN_SCALE * LOG2E)), "w_qnT": w_qn_t.astype(BF16), "w_qrT": w_qr_t.astype(BF16),
        "g_kvn": row(g_kvn[l]), "w_k": w_k.astype(BF16), "w_vT": w_v_t.astype(BF16),
        "w_a2": w_a2_p.astype(BF16), "b_a2": row(b_a2[l]),
        "g_on": row(g_gla_on[l]), "w_o": w_o[l].astype(BF16), "g_ffn": row(g_ffn[l]),
        "w_g": w_gu[l][:, :d_ff].astype(BF16), "w_u": w_gu[l][:, d_ff:].astype(BF16),
        "w_down": w_down[l].astype(BF16),
    }


def _rope_tables(pos):
    inv = ROPE_THETA ** (-jnp.arange(ROPE_HALF, dtype=F32) / ROPE_HALF)
    ang = pos.astype(F32)[:, None] * inv[None, :]
    cos, sin = jnp.cos(ang), jnp.sin(ang)
    zeros = jnp.zeros((pos.shape[0], LANES - QK_ROPE), F32)
    cr = jnp.concatenate([cos, cos, zeros], axis=1)
    sr = jnp.concatenate([-sin, sin, zeros], axis=1)
    ct = jnp.tile(cos, (1, MLA_HEADS)).T
    st = jnp.tile(sin, (1, MLA_HEADS)).T
    return cr, sr, ct, st


def _pick(n, pref):
    for t in pref:
        if n % t == 0:
            return t
    return n


def kernel(x_prompt, x_sample, cache_ckv, cache_krope, state_gla, g_attn, w_in, g_qn, w_uq, g_kvn, w_ukv, w_a2, b_a2,
           g_gla_on, w_o, g_ffn, w_gu, w_down, g_final):
    depth = w_in.shape[0]
    B, T, D = x_prompt.shape
    nb, ts, _ = x_sample.shape
    past = cache_ckv.shape[2]
    assert T % CHUNK == 0 and ts % BF16_ROWS == 0 and ts <= CHUNK

    tabs_p = _rope_tables(jnp.arange(T, dtype=jnp.int32))
    tabs_s = _rope_tables(jnp.tile(past + jnp.arange(ts, dtype=jnp.int32), nb))
    tm_p = _pick(T, (512, 256, 128))
    tk = _pick(T, (ATTN_TK, 256, 128, 64))
    sw = ATTN_SW if (T % ATTN_SW == 0 and ATTN_SW % tk == 0) else tk
    tq = ATTN_TQ if (T % ATTN_TQ == 0 and ATTN_TQ % sw == 0) else sw
    tg = _pick(T, (512, 256, 128, 64))
    g_fin = g_final.reshape(1, -1)

    xp = x_prompt
    xs = x_sample.reshape(1, nb * ts, D)
    outs = {k: [] for k in ("ckv_p", "kr_p", "st_p", "ckv_s", "kr_s", "st_s")}
    for l in range(depth):
        lw = _layer_weights(l, g_attn, w_in, g_qn, w_uq, g_kvn, w_ukv, w_a2, b_a2, g_gla_on, w_o, g_ffn, w_gu,
                            w_down)
        final = l == depth - 1
        ckv, kr, qt, k, vt, qn2, kn2, gq, gk, gv, gg, lg = _proj(xp, tabs_p, lw, tm_p, True)
        o_mla = _attn(qt, k, vt, qn2[:, :, :MLA_HEADS, 0], kn2[:, :, 0, :MLA_HEADS], tq, tk, sw, 0)
        s0 = jnp.zeros((B, GLA_DV, GLA_K), F32)
        o_g, st_fin = _gla(gq, gk, lg, gv, gg, lw["g_on"], s0, CHUNK, tg)
        xp = _ffn(xp, o_mla, o_g, lw, g_fin, tm_p, final)
        outs["ckv_p"].append(ckv), outs["kr_p"].append(kr)
        outs["st_p"].append(st_fin.reshape(B, GLA_DV, GLA_HEADS, GLA_DK).transpose(0, 2, 3, 1))
        ckv, kr, qt, gq, gk, gv, gg, lg = _proj(xs, tabs_s, lw, nb * ts, False)
        ckv, kr = ckv.reshape(nb, ts, -1), kr.reshape(nb, ts, -1)
        k, vt = _kvproj(cache_ckv[l], cache_krope[l], ckv, kr, lw)
        qt = qt.reshape(-1, nb, ts).transpose(1, 0, 2)
        no_norms = jnp.zeros((nb, 1, MLA_HEADS), F32)
        o_mla = _attn(qt, k, vt, no_norms, no_norms, ts, past + ts, ts, past).reshape(1, nb * ts, -1)
        s0 = state_gla[l].astype(F32).transpose(0, 3, 1, 2).reshape(nb, GLA_DV, GLA_K)
        per_b = lambda a: a.reshape(nb, ts, -1)
        o_g, st_fin = _gla(per_b(gq), per_b(gk), per_b(lg), per_b(gv), per_b(gg), lw["g_on"], s0, ts, ts)
        xs = _ffn(xs, o_mla, o_g.reshape(1, nb * ts, -1), lw, g_fin, nb * ts, final)
        outs["ckv_s"].append(ckv), outs["kr_s"].append(kr)
        outs["st_s"].append(st_fin.reshape(nb, GLA_DV, GLA_HEADS, GLA_DK).transpose(0, 2, 3, 1))

    stack = lambda key: jnp.stack(outs[key])
    return (xp, xs.reshape(nb, ts, D), stack("ckv_p"), stack("kr_p"), stack("st_p"),
            stack("ckv_s"), stack("kr_s"), stack("st_s"))
```

```python
import functools

import jax
import jax.numpy as jnp
import numpy as np
from jax import lax
from jax.experimental import pallas as pl
from jax.experimental.pallas import tpu as pltpu

F32 = jnp.float32
BF16 = jnp.bfloat16

CHUNK = 64
EPS = 1e-6
MLA_HEADS = 8
QK_NOPE = 64
QK_ROPE = 32
ROPE_HALF = QK_ROPE // 2
V_HEAD = 64
Q_LORA = 256
KV_LORA = 256
ROPE_THETA = 10000.0
ATTN_SCALE = (QK_NOPE + QK_ROPE) ** -0.5
LOG2E = 1.4426950408889634
GLA_HEADS = 4
GLA_DK = 64
GLA_DV = 128
GATE_LR = 16
GATE_TAU = 16.0
GLA_K = GLA_HEADS * GLA_DK
GLA_V = GLA_HEADS * GLA_DV

LANES = 128
BF16_ROWS = 16
HEAD_PAD = 128
PV_ROWS = 256
ATTN_TK = 512
ATTN_SW = 512
ATTN_TQ = 1024
SUB = 8
NEG = -1e30
EXP_RANGE = 64.0
NORM_MARGIN = 1.05
VMEM_LIMIT = 48 * 1024 * 1024

Z_CQ = 0
Z_CKV = Z_CQ + Q_LORA
Z_GQ = Z_CKV + KV_LORA
Z_GK = Z_GQ + GLA_K
Z_GV = Z_GK + GLA_K
Z_GG = Z_GV + GLA_V
Z_SM = Z_GG + GLA_V
Z_W = Z_SM + LANES


def _nt(a, b):
    return lax.dot_general(a, b, (((1,), (1,)), ((), ())), preferred_element_type=F32)


def _tn(a, b):
    return lax.dot_general(a, b, (((0,), (0,)), ((), ())), preferred_element_type=F32)


def _dot(a, b):
    return jnp.dot(a, b, preferred_element_type=F32)


def _rms(x, g):
    return x * lax.rsqrt(jnp.mean(x * x, axis=-1, keepdims=True) + EPS) * g


def _proj_kernel(emit_kv, x_ref, cr_ref, sr_ref, ct_ref, st_ref, gat_ref, win_ref, gqn_ref, wqn_ref, wqr_ref,
                 gkv_ref, wk_ref, wvt_ref, wa2_ref, ba2_ref, eq_ref, ek_ref, *outs):
    if emit_kv:
        ckv_ref, kr_ref, qt_ref, k_ref, vt_ref, qn2_ref, kn2_ref, gq_ref, gk_ref, gv_ref, gg_ref, lg_ref = outs
    else:
        ckv_ref, kr_ref, qt_ref, gq_ref, gk_ref, gv_ref, gg_ref, lg_ref = outs
    x = x_ref[0]
    h = _rms(x, gat_ref[...]).astype(BF16)
    z = _dot(h, win_ref[...])

    cqn = _rms(z[:, Z_CQ:Z_CQ + Q_LORA], gqn_ref[...]).astype(BF16)
    n_nope = MLA_HEADS * QK_NOPE
    n_rope = MLA_HEADS * ROPE_HALF
    q_nope = _nt(wqn_ref[...], cqn)
    qr = _nt(wqr_ref[...], cqn)
    x1, x2 = qr[:n_rope], qr[n_rope:]
    ct, st = ct_ref[...], st_ref[...]
    q_r1, q_r2 = x1 * ct - x2 * st, x1 * st + x2 * ct
    tm = q_nope.shape[1]
    for hd in range(MLA_HEADS):
        qt_ref[0, hd, 0:QK_NOPE, :] = q_nope[hd * QK_NOPE:(hd + 1) * QK_NOPE].astype(BF16)
        qt_ref[0, hd, QK_NOPE:QK_NOPE + ROPE_HALF, :] = q_r1[hd * ROPE_HALF:(hd + 1) * ROPE_HALF].astype(BF16)
        qt_ref[0, hd, QK_NOPE + ROPE_HALF:QK_NOPE + QK_ROPE, :] = q_r2[hd * ROPE_HALF:(hd + 1) * ROPE_HALF].astype(BF16)
        qt_ref[0, hd, QK_NOPE + QK_ROPE:, :] = jnp.zeros((HEAD_PAD - QK_NOPE - QK_ROPE, tm), BF16)

    ckv = _rms(z[:, Z_CKV:Z_CKV + KV_LORA], gkv_ref[...])
    ckv_ref[0] = ckv
    u = z[:, Z_SM:Z_SM + LANES]
    lane = lax.broadcasted_iota(jnp.int32, u.shape, 1)
    partner = jnp.where(lane < ROPE_HALF, pltpu.roll(u, LANES - ROPE_HALF, 1), pltpu.roll(u, ROPE_HALF, 1))
    kr = u * cr_ref[...] + partner * sr_ref[...]
    kr_ref[0] = kr[:, :QK_ROPE]
    if emit_kv:
        ckv_b = ckv.astype(BF16)
        k_all = _dot(jnp.concatenate([ckv_b, kr.astype(BF16)], axis=1), wk_ref[...])
        vt = _nt(wvt_ref[...], ckv_b)
        for hd in range(MLA_HEADS):
            k_ref[0, hd] = k_all[:, hd * HEAD_PAD:(hd + 1) * HEAD_PAD].astype(BF16)
            vt_ref[0, hd] = vt[hd * V_HEAD:(hd + 1) * V_HEAD].astype(BF16)
        q_sq = jnp.concatenate([q_nope * q_nope, q_r1 * q_r1, q_r2 * q_r2], axis=0).astype(BF16)
        qn2 = jnp.max(_dot(eq_ref[...], q_sq), axis=1, keepdims=True)
        qn2_ref[0, 0] = jnp.broadcast_to(qn2, qn2_ref.shape[2:])
        kn2 = _dot((k_all * k_all).astype(BF16), ek_ref[...])
        kn2_ref[0, 0] = jnp.max(kn2, axis=0, keepdims=True)

    gq_ref[0] = z[:, Z_GQ:Z_GQ + GLA_K]
    gk_ref[0] = z[:, Z_GK:Z_GK + GLA_K]
    gv_ref[0] = z[:, Z_GV:Z_GV + GLA_V].astype(BF16)
    gg_ref[0] = z[:, Z_GG:Z_GG + GLA_V]
    xg = _dot(u.astype(BF16), wa2_ref[...]) + ba2_ref[...]
    lg_ref[0] = (jnp.minimum(xg, 0.0) - jnp.log1p(jnp.exp(-jnp.abs(xg)))) * (1.0 / GATE_TAU)


def _proj(x, tabs, lw, tm, emit_kv):
    B, T, D = x.shape
    nt = T // tm
    cr, sr, ct, st = tabs
    tok = lambda w: pl.BlockSpec((1, tm, w), lambda b, t: (b, t, 0))
    full = lambda a: pl.BlockSpec(a.shape, lambda b, t: (0,) * a.ndim)
    weights = (lw["g_attn"], lw["w_in"], lw["g_qn"], lw["w_qnT"], lw["w_qrT"], lw["g_kvn"], lw["w_k"], lw["w_vT"],
               lw["w_a2"], lw["b_a2"], lw["e_q"], lw["e_k"])
    in_specs = [tok(D),
                pl.BlockSpec((tm, LANES), lambda b, t: (t, 0)), pl.BlockSpec((tm, LANES), lambda b, t: (t, 0)),
                pl.BlockSpec((LANES, tm), lambda b, t: (0, t)), pl.BlockSpec((LANES, tm), lambda b, t: (0, t))]
    in_specs += [full(w) for w in weights]
    S = jax.ShapeDtypeStruct
    head_feat = lambda r: pl.BlockSpec((1, MLA_HEADS, r, tm), lambda b, t: (b, 0, 0, t))
    out_shape = [S((B, T, KV_LORA), F32), S((B, T, QK_ROPE), F32), S((B, MLA_HEADS, HEAD_PAD, T), BF16)]
    out_specs = [tok(KV_LORA), tok(QK_ROPE), head_feat(HEAD_PAD)]
    if emit_kv:
        out_shape += [S((B, MLA_HEADS, T, HEAD_PAD), BF16), S((B, MLA_HEADS, V_HEAD, T), BF16),
                      S((B, nt, BF16_ROWS, LANES), F32), S((B, nt, 1, LANES), F32)]
        out_specs += [pl.BlockSpec((1, MLA_HEADS, tm, HEAD_PAD), lambda b, t: (b, 0, t, 0)), head_feat(V_HEAD),
                      pl.BlockSpec((1, 1, BF16_ROWS, LANES), lambda b, t: (b, t, 0, 0)),
                      pl.BlockSpec((1, 1, 1, LANES), lambda b, t: (b, t, 0, 0))]
    out_shape += [S((B, T, GLA_K), F32), S((B, T, GLA_K), F32), S((B, T, GLA_V), BF16), S((B, T, GLA_V), F32),
                  S((B, T, GLA_K), F32)]
    out_specs += [tok(GLA_K), tok(GLA_K), tok(GLA_V), tok(GLA_V), tok(GLA_K)]
    return pl.pallas_call(
        functools.partial(_proj_kernel, emit_kv),
        grid=(B, nt), in_specs=in_specs, out_specs=out_specs, out_shape=out_shape,
        compiler_params=pltpu.CompilerParams(dimension_semantics=("parallel", "parallel"),
                                             vmem_limit_bytes=VMEM_LIMIT),
        name="proj_kv" if emit_kv else "proj",
    )(x, cr, sr, ct, st, *weights)


def _cache_attn_kernel(past, cckv_ref, ckr_ref, nckv_ref, nkr_ref, qt_ref, wuk_ref, wuv_ref, o_ref):
    ts = qt_ref.shape[-1]
    ckv = jnp.concatenate([cckv_ref[0], nckv_ref[0]], axis=0).astype(BF16)
    kr = jnp.concatenate([ckr_ref[0], nkr_ref[0]], axis=0).astype(BF16)
    tot = ckv.shape[0]
    eye = (lax.broadcasted_iota(jnp.int32, (QK_ROPE, QK_ROPE), 0)
           == lax.broadcasted_iota(jnp.int32, (QK_ROPE, QK_ROPE), 1)).astype(BF16)
    q_lat, q_rope = [], []
    for h in range(MLA_HEADS):
        q_lat.append(_tn(qt_ref[0, h, 0:QK_NOPE, :], wuk_ref[h]).astype(BF16))
        q_rope.append(_tn(qt_ref[0, h, QK_NOPE:QK_NOPE + QK_ROPE, :], eye).astype(BF16))
    s = _nt(jnp.concatenate(q_lat, axis=0), ckv) + _nt(jnp.concatenate(q_rope, axis=0), kr)
    kpos = lax.broadcasted_iota(jnp.int32, s.shape, 1)
    qpos = past + lax.broadcasted_iota(jnp.int32, s.shape, 0) % ts
    s = jnp.where((kpos // CHUNK) <= (qpos // CHUNK), s, NEG)
    p = jnp.exp2(s - jnp.max(s, axis=1, keepdims=True))
    o_lat = _dot(p.astype(BF16), ckv) / jnp.sum(p, axis=1, keepdims=True)
    o = _dot(o_lat[0:ts].astype(BF16), wuv_ref[0])
    for h in range(1, MLA_HEADS):
        o = o + _dot(o_lat[h * ts:(h + 1) * ts].astype(BF16), wuv_ref[h])
    o_ref[0] = o.astype(BF16)


def _cache_attn(cache_ckv, cache_kr, new_ckv, new_kr, qt, lw):
    nb, past, _ = cache_ckv.shape
    ts = new_ckv.shape[1]
    per_b = lambda a: pl.BlockSpec((1,) + a.shape[1:], lambda b: (b,) + (0,) * (a.ndim - 1))
    full = lambda a: pl.BlockSpec(a.shape, lambda b: (0,) * a.ndim)
    return pl.pallas_call(
        functools.partial(_cache_attn_kernel, past), grid=(nb,),
        in_specs=[per_b(cache_ckv), per_b(cache_kr), per_b(new_ckv), per_b(new_kr), per_b(qt), full(lw["w_ukT"]),
                  full(lw["w_uv_pad"])],
        out_specs=pl.BlockSpec((1, ts, MLA_HEADS * V_HEAD), lambda b: (b, 0, 0)),
        out_shape=jax.ShapeDtypeStruct((nb, ts, MLA_HEADS * V_HEAD), BF16),
        compiler_params=pltpu.CompilerParams(dimension_semantics=("parallel",), vmem_limit_bytes=VMEM_LIMIT),
        name="cache_attn",
    )(cache_ckv, cache_kr, new_ckv, new_kr, qt, lw["w_ukT"], lw["w_uv_pad"])


def _attn_kernel(tq, tk, sw, q_off, norm_tiles, qi_ref, ki_ref, last_ref, qn2_ref, kn2_ref, qt_ref, k_ref, vt_ref,
                 o_ref, m_ref, l_ref, acc_ref, s_ref, p_ref, mmin_ref):
    nq_a, nk_a, nt_a = norm_tiles
    p_id = pl.program_id(1)
    qi, ki = qi_ref[p_id], ki_ref[p_id]

    @pl.when(ki == 0)
    def _():
        m_ref[...] = jnp.full(m_ref.shape, NEG, F32)
        l_ref[...] = jnp.zeros(l_ref.shape, F32)
        acc_ref[...] = jnp.zeros(acc_ref.shape, F32)

    rb = PV_ROWS if tk % PV_ROWS == 0 else tk
    ones_rows = jnp.ones((BF16_ROWS, rb), BF16)

    n_strips = tq // sw

    def qk(h, i, masked):
        s = _dot(k_ref[0, h], qt_ref[0, h, :, i * sw:(i + 1) * sw])
        if masked:
            kpos = ki * tk + lax.broadcasted_iota(jnp.int32, (tk, sw), 0)
            qpos = q_off + qi * tq + i * sw + lax.broadcasted_iota(jnp.int32, (tk, sw), 1)
            s = jnp.where((kpos // CHUNK) <= (qpos // CHUNK), s, NEG)
        return s

    def pv_rows(h, r):
        return jnp.concatenate([vt_ref[0, h, :, r * rb:(r + 1) * rb], ones_rows], axis=0)

    def over_heads(produce, consume):
        carried = produce(0, 0)
        for h in range(MLA_HEADS):
            nxt = produce(h + 1, (h + 1) % 2) if h + 1 < MLA_HEADS else None
            consume(h, h % 2, carried)
            carried = nxt

    def fast_step(modes):
        strips = [i for i in range(n_strips) if modes[i] != "skip"]

        def weights(h, slot):
            for i in strips:
                s = qk(h, i, modes[i] == "masked")
                p_ref[slot, i] = jnp.exp2(s - m_ref[h, :, i * sw:(i + 1) * sw]).astype(BF16)
            return 0

        def apply(h, slot, _):
            for i in strips:
                pv = None
                for r in range(tk // rb):
                    d = _dot(pv_rows(h, r), p_ref[slot, i, r * rb:(r + 1) * rb, :])
                    pv = d if pv is None else pv + d
                cols = slice(i * sw, (i + 1) * sw)
                acc_ref[h, :, cols] = acc_ref[h, :, cols] + pv[:V_HEAD]
                l_ref[h, :, cols] = l_ref[h, :, cols] + pv[V_HEAD:V_HEAD + 1]

        over_heads(weights, apply)

    def step(modes):
        strips = [i for i in range(n_strips) if modes[i] != "skip"]

        def scores(h, slot):
            out = []
            for i in strips:
                s = qk(h, i, modes[i] == "masked")
                s_ref[slot, i] = s
                out.append(jnp.max(s, axis=0, keepdims=True))
            return tuple(out)

        def apply(h, slot, smax):
            for n, i in enumerate(strips):
                cols = slice(i * sw, (i + 1) * sw)
                m_old = m_ref[h, :, cols]
                m_new = jnp.maximum(m_old, smax[n])
                alpha = jnp.exp2(m_old - m_new)
                m_ref[h, :, cols] = m_new
                pv = None
                for r in range(tk // rb):
                    p = jnp.exp2(s_ref[slot, i, r * rb:(r + 1) * rb, :] - m_new).astype(BF16)
                    d = _dot(pv_rows(h, r), p)
                    pv = d if pv is None else pv + d
                acc_ref[h, :, cols] = alpha * acc_ref[h, :, cols] + pv[:V_HEAD]
                l_ref[h, :, cols] = alpha * l_ref[h, :, cols] + pv[V_HEAD:V_HEAD + 1]

        over_heads(scores, apply)
        for h in range(MLA_HEADS):
            mmin_ref[h] = jnp.min(m_ref[h])

    if sw % tk == 0 and q_off % sw == 0:
        b = pl.program_id(0)
        fast = ki > 0
        for h in range(MLA_HEADS):
            q2 = qn2_ref[((b * nt_a + qi * nq_a) * MLA_HEADS) + h]
            for j in range(1, nq_a):
                q2 = jnp.maximum(q2, qn2_ref[((b * nt_a + qi * nq_a + j) * MLA_HEADS) + h])
            k2 = kn2_ref[((b * nt_a + ki * nk_a) * MLA_HEADS) + h]
            for j in range(1, nk_a):
                k2 = jnp.maximum(k2, kn2_ref[((b * nt_a + ki * nk_a + j) * MLA_HEADS) + h])
            room = mmin_ref[h] + EXP_RANGE
            fast = fast & (room > 0.0) & (q2 * k2 * NORM_MARGIN <= room * room)
        slow = jnp.logical_not(fast)
        per = sw // tk
        rel = ki - (q_off // tk + qi * n_strips * per)
        pl.when(fast & (rel < 0))(lambda: fast_step(["full"] * n_strips))
        for r in range(n_strips * per):
            modes = ["full" if r < i * per else "masked" if r < (i + 1) * per else "skip" for i in range(n_strips)]
            pl.when(fast & (rel == r))(functools.partial(fast_step, modes))
        pl.when(slow)(lambda: step(["masked"] * n_strips))
    else:
        step(["masked"] * n_strips)

    @pl.when(last_ref[p_id] == 1)
    def _():
        parts = [acc_ref[h] * (1.0 / l_ref[h]) for h in range(MLA_HEADS)]
        o_ref[0] = jnp.concatenate(parts, axis=0).T.astype(BF16)


def _attn(qt, k, vt, qn2, kn2, tq, tk, sw, q_off):
    B, _, _, Tq = qt.shape
    Tk = k.shape[2]
    nq, nk = Tq // tq, Tk // tk
    nt_a = qn2.shape[1]
    norm_tiles = (max(1, nt_a // nq), max(1, nt_a // nk), nt_a)
    qi_l, ki_l, last_l = [], [], []
    for qi in range(nq):
        last_chunk = (q_off + qi * tq + tq - 1) // CHUNK
        kmax = min(nk - 1, ((last_chunk + 1) * CHUNK - 1) // tk)
        for ki in range(kmax + 1):
            qi_l.append(qi), ki_l.append(ki), last_l.append(int(ki == kmax))
    tabs = [jnp.asarray(np.asarray(a, np.int32)) for a in (qi_l, ki_l, last_l)]
    tabs += [qn2.reshape(-1), kn2.reshape(-1)]
    grid_spec = pltpu.PrefetchScalarGridSpec(
        num_scalar_prefetch=5, grid=(B, len(qi_l)),
        in_specs=[pl.BlockSpec((1, MLA_HEADS, HEAD_PAD, tq), lambda b, p, qi, ki, *_: (b, 0, 0, qi[p])),
                  pl.BlockSpec((1, MLA_HEADS, tk, HEAD_PAD), lambda b, p, qi, ki, *_: (b, 0, ki[p], 0)),
                  pl.BlockSpec((1, MLA_HEADS, V_HEAD, tk), lambda b, p, qi, ki, *_: (b, 0, 0, ki[p]))],
        out_specs=pl.BlockSpec((1, tq, MLA_HEADS * V_HEAD), lambda b, p, qi, ki, *_: (b, qi[p], 0)),
        scratch_shapes=[pltpu.VMEM((MLA_HEADS, 1, tq), F32), pltpu.VMEM((MLA_HEADS, 1, tq), F32),
                        pltpu.VMEM((MLA_HEADS, V_HEAD, tq), F32), pltpu.VMEM((2, tq // sw, tk, sw), F32),
                        pltpu.VMEM((2, tq // sw, tk, sw), BF16), pltpu.SMEM((MLA_HEADS,), F32)])
    return pl.pallas_call(
        functools.partial(_attn_kernel, tq, tk, sw, q_off, norm_tiles), grid_spec=grid_spec,
        out_shape=jax.ShapeDtypeStruct((B, Tq, MLA_HEADS * V_HEAD), BF16),
        compiler_params=pltpu.CompilerParams(dimension_semantics=("parallel", "arbitrary"),
                                             vmem_limit_bytes=VMEM_LIMIT),
        name="attn",
    )(*tabs, qt, k, vt)


def _gla_kernel(lc, n_chunks, nch, gq_ref, gk_ref, lg_ref, gv_ref, gg_ref, gon_ref, s0_ref, og_ref, sfin_ref,
                st_ref):
    n_sub = lc // SUB
    n_stack = SUB * n_sub * (n_sub - 1) // 2
    stack_pad = -(-n_stack // LANES) * LANES

    @pl.when(pl.program_id(1) == 0)
    def _():
        st_ref[...] = s0_ref[0]

    lane_k = lax.broadcasted_iota(jnp.int32, (lc, GLA_K), 1) // GLA_DK
    ri = lax.broadcasted_iota(jnp.int32, (lc, lc), 0)
    ci = lax.broadcasted_iota(jnp.int32, (lc, lc), 1)
    tri = (ci <= ri).astype(BF16)
    e_r = lax.broadcasted_iota(jnp.int32, (GLA_K, GLA_V), 0) // GLA_DK
    e_c = lax.broadcasted_iota(jnp.int32, (GLA_K, GLA_V), 1) // GLA_DV
    e_mat = (e_r == e_c).astype(BF16)
    sub_row = lax.broadcasted_iota(jnp.int32, (SUB, GLA_K), 0)
    st_lane = lax.broadcasted_iota(jnp.int32, (GLA_DV, GLA_K), 1) // GLA_DK
    a_row = lax.broadcasted_iota(jnp.int32, (lc, stack_pad), 0) // SUB
    a_col = lax.broadcasted_iota(jnp.int32, (lc, stack_pad), 1)
    off_mask = jnp.zeros((lc, stack_pad), jnp.bool_)
    seg = 0
    for i_sub in range(1, n_sub):
        off_mask = off_mask | ((a_row == i_sub) & (a_col >= seg) & (a_col < seg + SUB * i_sub))
        seg += SUB * i_sub

    def heads_on_rows(a):
        return jnp.concatenate([jnp.where(lane_k == h, a, 0.0) for h in range(GLA_HEADS)], axis=0).astype(BF16)

    def load(c):
        r = pl.ds(pl.multiple_of(c * lc, lc), lc)
        lg = lg_ref[0, r, :]
        hi = lg.astype(BF16)
        r1 = lg - hi.astype(F32)
        mid = r1.astype(BF16)
        lo = (r1 - mid.astype(F32)).astype(BF16)
        b = _dot(tri, hi) + _dot(tri, mid) + _dot(tri, lo)
        return dict(r=r, q=gq_ref[0, r, :], k=gk_ref[0, r, :], v=gv_ref[0, r, :].astype(F32), b=b)

    def local(d):
        q, k, v, b = d["q"], d["k"], d["v"], d["b"]
        v_b = v.astype(BF16)
        b_last = b[lc - 1:lc, :]
        d["decay"] = jnp.exp(b_last)
        d["q4"] = heads_on_rows(q * jnp.exp(b))
        kd = (k * jnp.exp(b_last - b)).astype(BF16)
        full = _tn(v_b, kd)
        upd = jnp.zeros((GLA_DV, GLA_K), F32)
        for h in range(GLA_HEADS):
            upd = upd + jnp.where(st_lane == h, full[h * GLA_DV:(h + 1) * GLA_DV], 0.0)
        d["upd"] = upd
        refs = [jnp.zeros((SUB, GLA_K), F32)]
        kst, vst = [], []
        for i_sub in range(1, n_sub):
            r_i = b[SUB * i_sub - 1:SUB * i_sub, :]
            refs.append(jnp.broadcast_to(r_i, (SUB, GLA_K)))
            kst.append(k[:SUB * i_sub] * jnp.exp(r_i - b[:SUB * i_sub]))
            vst.append(v[:SUB * i_sub])
        if stack_pad > n_stack:
            kst.append(jnp.zeros((stack_pad - n_stack, GLA_K), F32))
            vst.append(jnp.zeros((stack_pad - n_stack, GLA_V), F32))
        kst = jnp.concatenate(kst, axis=0).astype(BF16)
        vst = jnp.concatenate(vst, axis=0).astype(BF16)
        qp = q * jnp.exp(b - jnp.concatenate(refs, axis=0))
        a4 = _nt(heads_on_rows(qp), kst)
        d["o_off"] = [
            _dot(jnp.where(off_mask, a4[h * lc:(h + 1) * lc], 0.0).astype(BF16), vst[:, h * GLA_DV:(h + 1) * GLA_DV])
            for h in range(GLA_HEADS)]

    def same_sub(d):
        q, k, v, b = d["q"], d["k"], d["v"], d["b"]
        terms = []
        for i_sub in range(n_sub):
            rs = slice(SUB * i_sub, SUB * (i_sub + 1))
            q_i, k_i, b_i = q[rs], k[rs], b[rs]
            for j in range(SUB):
                arg = jnp.where(sub_row >= j, b_i - b_i[j:j + 1], NEG)
                terms.append(q_i * (k_i[j:j + 1] * jnp.exp(arg)))
        w = _dot(jnp.concatenate(terms, axis=0).astype(BF16), e_mat)
        o_diag = []
        for i_sub in range(n_sub):
            acc = None
            for j in range(SUB):
                row = SUB * i_sub + j
                t = w[row * SUB:(row + 1) * SUB] * v[row:row + 1]
                acc = t if acc is None else acc + t
            o_diag.append(acc)
        d["o_diag"] = jnp.concatenate(o_diag, axis=0)

    def carried(d):
        st = st_ref[...]
        d["o_heads"] = _nt(d["q4"], st.astype(BF16))
        st_ref[...] = d["decay"] * st + d["upd"]

    def finish(d):
        o = jnp.concatenate(
            [d["o_heads"][h * lc:(h + 1) * lc] + d["o_off"][h] for h in range(GLA_HEADS)], axis=1) + d["o_diag"]
        gon = gon_ref[...]
        normed = jnp.concatenate(
            [_rms(o[:, h * GLA_DV:(h + 1) * GLA_DV], gon) for h in range(GLA_HEADS)], axis=1)
        gg = gg_ref[0, d["r"], :]
        og_ref[0, d["r"], :] = (normed * (gg * jax.nn.sigmoid(gg))).astype(BF16)

    def body(it, carry):
        ds = [load(it * nch + u) for u in range(nch)]
        for stage in (local, same_sub, carried, finish):
            for d in ds:
                stage(d)
        return carry

    lax.fori_loop(0, n_chunks // nch, body, 0)

    @pl.when(pl.program_id(1) == pl.num_programs(1) - 1)
    def _():
        sfin_ref[0] = st_ref[...]


def _gla(gq, gk, lg, gv, gg, g_on, s0t, lc, tg):
    B, T, _ = gq.shape
    ng = T // tg
    tok = lambda w: pl.BlockSpec((1, tg, w), lambda b, g: (b, g, 0))
    st_spec = pl.BlockSpec((1, GLA_DV, GLA_K), lambda b, g: (b, 0, 0))
    S = jax.ShapeDtypeStruct
    return pl.pallas_call(
        functools.partial(_gla_kernel, lc, tg // lc, _pick(tg // lc, (4, 2))), grid=(B, ng),
        in_specs=[tok(GLA_K), tok(GLA_K), tok(GLA_K), tok(GLA_V), tok(GLA_V),
                  pl.BlockSpec((1, GLA_DV), lambda b, g: (0, 0)), st_spec],
        out_specs=[tok(GLA_V), st_spec],
        out_shape=[S((B, T, GLA_V), BF16), S((B, GLA_DV, GLA_K), F32)],
        scratch_shapes=[pltpu.VMEM((GLA_DV, GLA_K), F32)],
        compiler_params=pltpu.CompilerParams(dimension_semantics=("parallel", "arbitrary"),
                                             vmem_limit_bytes=VMEM_LIMIT),
        name="gla",
    )(gq, gk, lg, gv, gg, g_on, s0t)


def _ffn_kernel(final, x_ref, om_ref, og_ref, wo_ref, gf_ref, wg_ref, wu_ref, wd_ref, gfin_ref, y_ref):
    cat = jnp.concatenate([om_ref[0], og_ref[0]], axis=1)
    x1 = x_ref[0] + _dot(cat, wo_ref[...])
    hn = _rms(x1, gf_ref[...]).astype(BF16)
    a = _dot(hn, wg_ref[...])
    u = _dot(hn, wu_ref[...])
    act = (a * jax.nn.sigmoid(a) * u).astype(BF16)
    x2 = x1 + _dot(act, wd_ref[...])
    y_ref[0] = _rms(x2, gfin_ref[...]) if final else x2


def _ffn(x, o_mla, o_g, lw, g_final, tm, final):
    B, T, D = x.shape
    tok = lambda w: pl.BlockSpec((1, tm, w), lambda b, t: (b, t, 0))
    const = lambda a: pl.BlockSpec(a.shape, lambda b, t: (0,) * a.ndim, pipeline_mode=pl.Buffered(1))
    d_ff = lw["w_down"].shape[0]
    half = lambda j: pl.BlockSpec((D, d_ff), lambda b, t: (0, j), pipeline_mode=pl.Buffered(1))
    weights = (lw["w_o"], lw["g_ffn"], lw["w_gu"], lw["w_gu"], lw["w_down"], g_final)
    w_specs = [const(lw["w_o"]), const(lw["g_ffn"]), half(0), half(1), const(lw["w_down"]), const(g_final)]
    return pl.pallas_call(
        functools.partial(_ffn_kernel, final), grid=(B, T // tm),
        in_specs=[tok(D), tok(o_mla.shape[-1]), tok(o_g.shape[-1])] + w_specs,
        out_specs=tok(D), out_shape=jax.ShapeDtypeStruct((B, T, D), F32),
        compiler_params=pltpu.CompilerParams(dimension_semantics=("parallel", "parallel"),
                                             vmem_limit_bytes=VMEM_LIMIT),
        name="ffn_final" if final else "ffn",
    )(x, o_mla, o_g, *weights)


def _layer_weights(l, g_attn, w_in, g_qn, w_uq, g_kvn, w_ukv, w_a2, b_a2, g_gla_on, w_o, g_ffn, w_gu, w_down):
    d_model = w_in.shape[1]
    wi = w_in[l]
    c = np.cumsum([0, Q_LORA, KV_LORA, QK_ROPE, GLA_K, GLA_K, GLA_V, GLA_V, GATE_LR])
    sm = jnp.concatenate([wi[:, c[2]:c[3]], wi[:, c[7]:c[8]],
                          jnp.zeros((d_model, LANES - QK_ROPE - GATE_LR), F32)], axis=1)
    w_in_r = jnp.concatenate([wi[:, c[0]:c[2]], wi[:, c[3]:c[4]] * (GLA_DK ** -0.5), wi[:, c[4]:c[7]], sm], axis=1)
    wq = w_uq[l].reshape(Q_LORA, MLA_HEADS, QK_NOPE + QK_ROPE)
    w_qn_t = wq[:, :, :QK_NOPE].reshape(Q_LORA, -1).T
    w_qr_t = jnp.concatenate([wq[:, :, QK_NOPE:QK_NOPE + ROPE_HALF].reshape(Q_LORA, -1),
                              wq[:, :, QK_NOPE + ROPE_HALF:].reshape(Q_LORA, -1)], axis=1).T
    wkv = w_ukv[l].reshape(KV_LORA, MLA_HEADS, QK_NOPE + V_HEAD)
    wk_nope = jnp.pad(wkv[:, :, :QK_NOPE], ((0, 0), (0, 0), (0, HEAD_PAD - QK_NOPE))).reshape(KV_LORA, -1)
    place = np.zeros((LANES, MLA_HEADS, HEAD_PAD), np.float32)
    for d in range(QK_ROPE):
        place[d, :, QK_NOPE + d] = 1.0
    w_k = jnp.concatenate([wk_nope, jnp.asarray(place.reshape(LANES, -1))], axis=0)
    w_v_t = wkv[:, :, QK_NOPE:].reshape(KV_LORA, -1).T
    w_uk_t = wkv[:, :, :QK_NOPE].transpose(1, 2, 0)
    w_uv_pad = jnp.zeros((MLA_HEADS, KV_LORA, MLA_HEADS * V_HEAD), F32)
    for hd in range(MLA_HEADS):
        w_uv_pad = w_uv_pad.at[hd, :, hd * V_HEAD:(hd + 1) * V_HEAD].set(wkv[:, hd, QK_NOPE:])
    w_a2_p = jnp.zeros((LANES, GLA_K), F32).at[QK_ROPE:QK_ROPE + GATE_LR].set(w_a2[l])
    row = lambda v: v.reshape(1, -1)
    n_nope, n_rope = MLA_HEADS * QK_NOPE, MLA_HEADS * ROPE_HALF
    e_q = np.zeros((BF16_ROWS, n_nope + 2 * n_rope), np.float32)
    e_k = np.zeros((MLA_HEADS * HEAD_PAD, LANES), np.float32)
    for h in range(MLA_HEADS):
        e_q[h, h * QK_NOPE:(h + 1) * QK_NOPE] = 1.0
        e_q[h, n_nope + h * ROPE_HALF:n_nope + (h + 1) * ROPE_HALF] = 1.0
        e_q[h, n_nope + n_rope + h * ROPE_HALF:n_nope + n_rope + (h + 1) * ROPE_HALF] = 1.0
        e_k[h * HEAD_PAD:(h + 1) * HEAD_PAD, h] = 1.0
    return {
        "e_q": jnp.asarray(e_q, BF16), "e_k": jnp.asarray(e_k, BF16),
        "g_attn": row(g_attn[l]), "w_in": w_in_r.astype(BF16),
        "g_qn": row(g_qn[l] * (ATTN_SCALE * LOG2E)), "w_qnT": w_qn_t.astype(BF16), "w_qrT": w_qr_t.astype(BF16),
        "g_kvn": row(g_kvn[l]), "w_k": w_k.astype(BF16), "w_vT": w_v_t.astype(BF16),
        "w_ukT": w_uk_t.astype(BF16), "w_uv_pad": w_uv_pad.astype(BF16),
        "w_a2": w_a2_p.astype(BF16), "b_a2": row(b_a2[l]),
        "g_on": row(g_gla_on[l]), "w_o": w_o[l].astype(BF16), "g_ffn": row(g_ffn[l]),
        "w_gu": w_gu[l].astype(BF16), "w_down": w_down[l].astype(BF16),
    }


def _rope_tables(pos):
    inv = ROPE_THETA ** (-jnp.arange(ROPE_HALF, dtype=F32) / ROPE_HALF)
    ang = pos.astype(F32)[:, None] * inv[None, :]
    cos, sin = jnp.cos(ang), jnp.sin(ang)
    zeros = jnp.zeros((pos.shape[0], LANES - QK_ROPE), F32)
    cr = jnp.concatenate([cos, cos, zeros], axis=1)
    sr = jnp.concatenate([-sin, sin, zeros], axis=1)
    ct = jnp.tile(cos, (1, MLA_HEADS)).T
    st = jnp.tile(sin, (1, MLA_HEADS)).T
    return cr, sr, ct, st


def _pick(n, pref):
    for t in pref:
        if n % t == 0:
            return t
    return n


def kernel(x_prompt, x_sample, cache_ckv, cache_krope, state_gla, g_attn, w_in, g_qn, w_uq, g_kvn, w_ukv, w_a2, b_a2,
           g_gla_on, w_o, g_ffn, w_gu, w_down, g_final):
    depth = w_in.shape[0]
    B, T, D = x_prompt.shape
    nb, ts, _ = x_sample.shape
    past = cache_ckv.shape[2]
    assert T % CHUNK == 0 and ts % BF16_ROWS == 0 and ts <= CHUNK

    tabs_p = _rope_tables(jnp.arange(T, dtype=jnp.int32))
    tabs_s = _rope_tables(jnp.tile(past + jnp.arange(ts, dtype=jnp.int32), nb))
    tm_p = _pick(T, (512, 256, 128))
    tk = _pick(T, (ATTN_TK, 256, 128, 64))
    sw = ATTN_SW if (T % ATTN_SW == 0 and ATTN_SW % tk == 0) else tk
    tq = ATTN_TQ if (T % ATTN_TQ == 0 and ATTN_TQ % sw == 0) else sw
    tg = _pick(T, (512, 256, 128, 64))
    g_fin = g_final.reshape(1, -1)

    xp = x_prompt
    xs = x_sample.reshape(1, nb * ts, D)
    outs = {k: [] for k in ("ckv_p", "kr_p", "st_p", "ckv_s", "kr_s", "st_s")}
    for l in range(depth):
        lw = _layer_weights(l, g_attn, w_in, g_qn, w_uq, g_kvn, w_ukv, w_a2, b_a2, g_gla_on, w_o, g_ffn, w_gu,
                            w_down)
        final = l == depth - 1
        ckv, kr, qt, k, vt, qn2, kn2, gq, gk, gv, gg, lg = _proj(xp, tabs_p, lw, tm_p, True)
        o_mla = _attn(qt, k, vt, qn2[:, :, :MLA_HEADS, 0], kn2[:, :, 0, :MLA_HEADS], tq, tk, sw, 0)
        s0 = jnp.zeros((B, GLA_DV, GLA_K), F32)
        o_g, st_fin = _gla(gq, gk, lg, gv, gg, lw["g_on"], s0, CHUNK, tg)
        xp = _ffn(xp, o_mla, o_g, lw, g_fin, tm_p, final)
        outs["ckv_p"].append(ckv), outs["kr_p"].append(kr)
        outs["st_p"].append(st_fin.reshape(B, GLA_DV, GLA_HEADS, GLA_DK).transpose(0, 2, 3, 1))
        ckv, kr, qt, gq, gk, gv, gg, lg = _proj(xs, tabs_s, lw, nb * ts, False)
        ckv, kr = ckv.reshape(nb, ts, -1), kr.reshape(nb, ts, -1)
        qt = qt.reshape(MLA_HEADS, HEAD_PAD, nb, ts).transpose(2, 0, 1, 3)
        o_mla = _cache_attn(cache_ckv[l], cache_krope[l], ckv, kr, qt, lw).reshape(1, nb * ts, -1)
        s0 = state_gla[l].astype(F32).transpose(0, 3, 1, 2).reshape(nb, GLA_DV, GLA_K)
        per_b = lambda a: a.reshape(nb, ts, -1)
        o_g, st_fin = _gla(per_b(gq), per_b(gk), per_b(lg), per_b(gv), per_b(gg), lw["g_on"], s0, ts, ts)
        xs = _ffn(xs, o_mla, o_g.reshape(1, nb * ts, -1), lw, g_fin, nb * ts, final)
        outs["ckv_s"].append(ckv), outs["kr_s"].append(kr)
        outs["st_s"].append(st_fin.reshape(nb, GLA_DV, GLA_HEADS, GLA_DK).transpose(0, 2, 3, 1))

    stack = lambda key: jnp.stack(outs[key])
    return (xp, xs.reshape(nb, ts, D), stack("ckv_p"), stack("kr_p"), stack("st_p"),
            stack("ckv_s"), stack("kr_s"), stack("st_s"))
```

```python
import functools

import jax
import jax.numpy as jnp
import numpy as np
from jax import lax
from jax.experimental import pallas as pl
from jax.experimental.pallas import tpu as pltpu

F32 = jnp.float32
BF16 = jnp.bfloat16

CHUNK = 64
EPS = 1e-6
MLA_HEADS = 8
QK_NOPE = 64
QK_ROPE = 32
ROPE_HALF = QK_ROPE // 2
V_HEAD = 64
Q_LORA = 256
KV_LORA = 256
ROPE_THETA = 10000.0
ATTN_SCALE = (QK_NOPE + QK_ROPE) ** -0.5
LOG2E = 1.4426950408889634
GLA_HEADS = 4
GLA_DK = 64
GLA_DV = 128
GATE_LR = 16
GATE_TAU = 16.0
GLA_K = GLA_HEADS * GLA_DK
GLA_V = GLA_HEADS * GLA_DV

LANES = 128
BF16_ROWS = 16
HEAD_PAD = 128
PV_ROWS = 256
ATTN_TK = 512
ATTN_TQ = 1024
SUB = 8
NEG = -1e30
EXP_RANGE = 64.0
NORM_MARGIN = 1.05
VMEM_LIMIT = 48 * 1024 * 1024

Z_CQ = 0
Z_CKV = Z_CQ + Q_LORA
Z_GQ = Z_CKV + KV_LORA
Z_GK = Z_GQ + GLA_K
Z_GV = Z_GK + GLA_K
Z_GG = Z_GV + GLA_V
Z_SM = Z_GG + GLA_V
Z_W = Z_SM + LANES


def _nt(a, b):
    return lax.dot_general(a, b, (((1,), (1,)), ((), ())), preferred_element_type=F32)


def _tn(a, b):
    return lax.dot_general(a, b, (((0,), (0,)), ((), ())), preferred_element_type=F32)


def _dot(a, b):
    return jnp.dot(a, b, preferred_element_type=F32)


def _rms(x, g):
    return x * lax.rsqrt(jnp.mean(x * x, axis=-1, keepdims=True) + EPS) * g


def _proj_kernel(emit_kv, x_ref, cr_ref, sr_ref, ct_ref, st_ref, gat_ref, win_ref, gqn_ref, wqn_ref, wqr_ref,
                 gkv_ref, wk_ref, wvt_ref, wa2_ref, ba2_ref, eq_ref, ek_ref, *outs):
    if emit_kv:
        ckv_ref, kr_ref, qt_ref, k_ref, vt_ref, qn2_ref, kn2_ref, gq_ref, gk_ref, gv_ref, gg_ref, lg_ref = outs
    else:
        ckv_ref, kr_ref, qt_ref, gq_ref, gk_ref, gv_ref, gg_ref, lg_ref = outs
    x = x_ref[0]
    h = _rms(x, gat_ref[...]).astype(BF16)
    z = _dot(h, win_ref[...])

    cqn = _rms(z[:, Z_CQ:Z_CQ + Q_LORA], gqn_ref[...]).astype(BF16)
    n_nope = MLA_HEADS * QK_NOPE
    n_rope = MLA_HEADS * ROPE_HALF
    q_nope = _nt(wqn_ref[...], cqn)
    qt_ref[0, 0:n_nope, :] = q_nope.astype(BF16)
    qr = _nt(wqr_ref[...], cqn)
    x1, x2 = qr[:n_rope], qr[n_rope:]
    ct, st = ct_ref[...], st_ref[...]
    q_r1, q_r2 = x1 * ct - x2 * st, x1 * st + x2 * ct
    qt_ref[0, n_nope:n_nope + n_rope, :] = q_r1.astype(BF16)
    qt_ref[0, n_nope + n_rope:, :] = q_r2.astype(BF16)

    ckv = _rms(z[:, Z_CKV:Z_CKV + KV_LORA], gkv_ref[...])
    ckv_ref[0] = ckv
    u = z[:, Z_SM:Z_SM + LANES]
    lane = lax.broadcasted_iota(jnp.int32, u.shape, 1)
    partner = jnp.where(lane < ROPE_HALF, pltpu.roll(u, LANES - ROPE_HALF, 1), pltpu.roll(u, ROPE_HALF, 1))
    kr = u * cr_ref[...] + partner * sr_ref[...]
    kr_ref[0] = kr[:, :QK_ROPE]
    if emit_kv:
        ckv_b = ckv.astype(BF16)
        k_all = _dot(jnp.concatenate([ckv_b, kr.astype(BF16)], axis=1), wk_ref[...])
        k_ref[0] = k_all.astype(BF16)
        vt_ref[0] = _nt(wvt_ref[...], ckv_b).astype(BF16)
        q_sq = jnp.concatenate([q_nope * q_nope, q_r1 * q_r1, q_r2 * q_r2], axis=0).astype(BF16)
        qn2 = jnp.max(_dot(eq_ref[...], q_sq), axis=1, keepdims=True)
        qn2_ref[0, 0] = jnp.broadcast_to(qn2, qn2_ref.shape[2:])
        kn2 = _dot((k_all * k_all).astype(BF16), ek_ref[...])
        kn2_ref[0, 0] = jnp.max(kn2, axis=0, keepdims=True)

    gq_ref[0] = z[:, Z_GQ:Z_GQ + GLA_K]
    gk_ref[0] = z[:, Z_GK:Z_GK + GLA_K]
    gv_ref[0] = z[:, Z_GV:Z_GV + GLA_V].astype(BF16)
    gg_ref[0] = z[:, Z_GG:Z_GG + GLA_V]
    xg = _dot(u.astype(BF16), wa2_ref[...]) + ba2_ref[...]
    lg_ref[0] = (jnp.minimum(xg, 0.0) - jnp.log1p(jnp.exp(-jnp.abs(xg)))) * (1.0 / GATE_TAU)


def _proj(x, tabs, lw, tm, emit_kv):
    B, T, D = x.shape
    nt = T // tm
    cr, sr, ct, st = tabs
    tok = lambda w: pl.BlockSpec((1, tm, w), lambda b, t: (b, t, 0))
    feat = lambda r: pl.BlockSpec((1, r, tm), lambda b, t: (b, 0, t))
    full = lambda a: pl.BlockSpec(a.shape, lambda b, t: (0,) * a.ndim)
    weights = (lw["g_attn"], lw["w_in"], lw["g_qn"], lw["w_qnT"], lw["w_qrT"], lw["g_kvn"], lw["w_k"], lw["w_vT"],
               lw["w_a2"], lw["b_a2"], lw["e_q"], lw["e_k"])
    in_specs = [tok(D),
                pl.BlockSpec((tm, LANES), lambda b, t: (t, 0)), pl.BlockSpec((tm, LANES), lambda b, t: (t, 0)),
                pl.BlockSpec((LANES, tm), lambda b, t: (0, t)), pl.BlockSpec((LANES, tm), lambda b, t: (0, t))]
    in_specs += [full(w) for w in weights]
    S = jax.ShapeDtypeStruct
    nq = MLA_HEADS * (QK_NOPE + QK_ROPE)
    out_shape = [S((B, T, KV_LORA), F32), S((B, T, QK_ROPE), F32), S((B, nq, T), BF16)]
    out_specs = [tok(KV_LORA), tok(QK_ROPE), feat(nq)]
    if emit_kv:
        out_shape += [S((B, T, MLA_HEADS * HEAD_PAD), BF16), S((B, MLA_HEADS * V_HEAD, T), BF16),
                      S((B, nt, BF16_ROWS, LANES), F32), S((B, nt, 1, LANES), F32)]
        out_specs += [tok(MLA_HEADS * HEAD_PAD), feat(MLA_HEADS * V_HEAD),
                      pl.BlockSpec((1, 1, BF16_ROWS, LANES), lambda b, t: (b, t, 0, 0)),
                      pl.BlockSpec((1, 1, 1, LANES), lambda b, t: (b, t, 0, 0))]
    out_shape += [S((B, T, GLA_K), F32), S((B, T, GLA_K), F32), S((B, T, GLA_V), BF16), S((B, T, GLA_V), F32),
                  S((B, T, GLA_K), F32)]
    out_specs += [tok(GLA_K), tok(GLA_K), tok(GLA_V), tok(GLA_V), tok(GLA_K)]
    return pl.pallas_call(
        functools.partial(_proj_kernel, emit_kv),
        grid=(B, nt), in_specs=in_specs, out_specs=out_specs, out_shape=out_shape,
        compiler_params=pltpu.CompilerParams(dimension_semantics=("parallel", "parallel"),
                                             vmem_limit_bytes=VMEM_LIMIT),
        name="proj_kv" if emit_kv else "proj",
    )(x, cr, sr, ct, st, *weights)


def _cache_attn_kernel(past, cckv_ref, ckr_ref, nckv_ref, nkr_ref, qt_ref, wuk_ref, wuv_ref, o_ref):
    ts = qt_ref.shape[-1]
    ckv = jnp.concatenate([cckv_ref[0], nckv_ref[0]], axis=0).astype(BF16)
    kr = jnp.concatenate([ckr_ref[0], nkr_ref[0]], axis=0).astype(BF16)
    eye = (lax.broadcasted_iota(jnp.int32, (QK_ROPE, QK_ROPE), 0)
           == lax.broadcasted_iota(jnp.int32, (QK_ROPE, QK_ROPE), 1)).astype(BF16)
    q_lat, q_rope = [], []
    for h in range(MLA_HEADS):
        q_lat.append(_tn(qt_ref[0, h, 0:QK_NOPE, :], wuk_ref[h]).astype(BF16))
        q_rope.append(_tn(qt_ref[0, h, QK_NOPE:QK_NOPE + QK_ROPE, :], eye).astype(BF16))
    s = _nt(jnp.concatenate(q_lat, axis=0), ckv) + _nt(jnp.concatenate(q_rope, axis=0), kr)
    kpos = lax.broadcasted_iota(jnp.int32, s.shape, 1)
    qpos = past + lax.broadcasted_iota(jnp.int32, s.shape, 0) % ts
    s = jnp.where((kpos // CHUNK) <= (qpos // CHUNK), s, NEG)
    p = jnp.exp2(s - jnp.max(s, axis=1, keepdims=True))
    o_lat = _dot(p.astype(BF16), ckv) / jnp.sum(p, axis=1, keepdims=True)
    o = _dot(o_lat[0:ts].astype(BF16), wuv_ref[0])
    for h in range(1, MLA_HEADS):
        o = o + _dot(o_lat[h * ts:(h + 1) * ts].astype(BF16), wuv_ref[h])
    o_ref[0] = o.astype(BF16)


def _cache_attn(cache_ckv, cache_kr, new_ckv, new_kr, qt, lw):
    nb, past, _ = cache_ckv.shape
    ts = new_ckv.shape[1]
    per_b = lambda a: pl.BlockSpec((1,) + a.shape[1:], lambda b: (b,) + (0,) * (a.ndim - 1))
    full = lambda a: pl.BlockSpec(a.shape, lambda b: (0,) * a.ndim)
    return pl.pallas_call(
        functools.partial(_cache_attn_kernel, past), grid=(nb,),
        in_specs=[per_b(cache_ckv), per_b(cache_kr), per_b(new_ckv), per_b(new_kr), per_b(qt), full(lw["w_ukT"]),
                  full(lw["w_uv_pad"])],
        out_specs=pl.BlockSpec((1, ts, MLA_HEADS * V_HEAD), lambda b: (b, 0, 0)),
        out_shape=jax.ShapeDtypeStruct((nb, ts, MLA_HEADS * V_HEAD), BF16),
        compiler_params=pltpu.CompilerParams(dimension_semantics=("parallel",), vmem_limit_bytes=VMEM_LIMIT),
        name="cache_attn",
    )(cache_ckv, cache_kr, new_ckv, new_kr, qt, lw["w_ukT"], lw["w_uv_pad"])


def _attn_kernel(tq, tk, sw, q_off, norm_tiles, qi_ref, ki_ref, last_ref, qn2_ref, kn2_ref, qt_ref, k_ref, vt_ref,
                 o_ref, m_ref, l_ref, acc_ref, s_ref, p_ref, mmin_ref):
    nq_a, nk_a, nt_a = norm_tiles
    p_id = pl.program_id(1)
    qi, ki = qi_ref[p_id], ki_ref[p_id]
    n_nope = MLA_HEADS * QK_NOPE
    n_rope = MLA_HEADS * ROPE_HALF

    @pl.when(ki == 0)
    def _():
        m_ref[...] = jnp.full(m_ref.shape, NEG, F32)
        l_ref[...] = jnp.zeros(l_ref.shape, F32)
        acc_ref[...] = jnp.zeros(acc_ref.shape, F32)

    rb = PV_ROWS if tk % PV_ROWS == 0 else tk
    ones_rows = jnp.ones((BF16_ROWS, rb), BF16)

    n_strips = tq // sw

    def qk(h, i, masked):
        cols = slice(i * sw, (i + 1) * sw)
        qt_h = jnp.concatenate(
            [qt_ref[0, h * QK_NOPE:(h + 1) * QK_NOPE, cols],
             qt_ref[0, n_nope + h * ROPE_HALF:n_nope + (h + 1) * ROPE_HALF, cols],
             qt_ref[0, n_nope + n_rope + h * ROPE_HALF:n_nope + n_rope + (h + 1) * ROPE_HALF, cols],
             jnp.zeros((HEAD_PAD - QK_NOPE - QK_ROPE, sw), BF16)], axis=0)
        s = _dot(k_ref[0, :, h * HEAD_PAD:(h + 1) * HEAD_PAD], qt_h)
        if masked:
            kpos = ki * tk + lax.broadcasted_iota(jnp.int32, (tk, sw), 0)
            qpos = q_off + qi * tq + i * sw + lax.broadcasted_iota(jnp.int32, (tk, sw), 1)
            s = jnp.where((kpos // CHUNK) <= (qpos // CHUNK), s, NEG)
        return s

    def pv_rows(h, r):
        return jnp.concatenate([vt_ref[0, h * V_HEAD:(h + 1) * V_HEAD, r * rb:(r + 1) * rb], ones_rows], axis=0)

    def fast_step(modes):
        stages = [(h, i) for h in range(MLA_HEADS) for i in range(n_strips) if modes[i] != "skip"]

        def weights(t):
            h, i = stages[t]
            s = qk(h, i, modes[i] == "masked")
            p_ref[t % 2] = jnp.exp2(s - m_ref[h:h + 1, i * sw:(i + 1) * sw]).astype(BF16)

        weights(0)
        for t, (h, i) in enumerate(stages):
            if t + 1 < len(stages):
                weights(t + 1)
            pv = None
            for r in range(tk // rb):
                d = _dot(pv_rows(h, r), p_ref[t % 2, r * rb:(r + 1) * rb, :])
                pv = d if pv is None else pv + d
            rows, cols = slice(h * V_HEAD, (h + 1) * V_HEAD), slice(i * sw, (i + 1) * sw)
            acc_ref[rows, cols] = acc_ref[rows, cols] + pv[:V_HEAD]
            l_ref[h:h + 1, cols] = l_ref[h:h + 1, cols] + pv[V_HEAD:V_HEAD + 1]

    def step(modes):
        stages = [(h, i) for h in range(MLA_HEADS) for i in range(n_strips) if modes[i] != "skip"]

        def scores(t):
            h, i = stages[t]
            s = qk(h, i, modes[i] == "masked")
            s_ref[t % 2] = s
            return jnp.max(s, axis=0, keepdims=True)

        smax_next = scores(0)
        for t, (h, i) in enumerate(stages):
            smax = smax_next
            if t + 1 < len(stages):
                smax_next = scores(t + 1)
            cols = slice(i * sw, (i + 1) * sw)
            m_old = m_ref[h:h + 1, cols]
            m_new = jnp.maximum(m_old, smax)
            alpha = jnp.exp2(m_old - m_new)
            m_ref[h:h + 1, cols] = m_new
            rows = slice(h * V_HEAD, (h + 1) * V_HEAD)
            pv = None
            for r in range(tk // rb):
                p = jnp.exp2(s_ref[t % 2, r * rb:(r + 1) * rb, :] - m_new).astype(BF16)
                d = _dot(pv_rows(h, r), p)
                pv = d if pv is None else pv + d
            acc_ref[rows, cols] = alpha * acc_ref[rows, cols] + pv[:V_HEAD]
            l_ref[h:h + 1, cols] = alpha * l_ref[h:h + 1, cols] + pv[V_HEAD:V_HEAD + 1]
        for h in range(MLA_HEADS):
            mmin_ref[h] = jnp.min(m_ref[h:h + 1, :])

    if tk == sw and q_off % sw == 0:
        b = pl.program_id(0)
        fast = ki > 0
        for h in range(MLA_HEADS):
            q2 = qn2_ref[((b * nt_a + qi * nq_a) * MLA_HEADS) + h]
            for j in range(1, nq_a):
                q2 = jnp.maximum(q2, qn2_ref[((b * nt_a + qi * nq_a + j) * MLA_HEADS) + h])
            k2 = kn2_ref[((b * nt_a + ki * nk_a) * MLA_HEADS) + h]
            for j in range(1, nk_a):
                k2 = jnp.maximum(k2, kn2_ref[((b * nt_a + ki * nk_a + j) * MLA_HEADS) + h])
            room = mmin_ref[h] + EXP_RANGE
            fast = fast & (room > 0.0) & (q2 * k2 * NORM_MARGIN <= room * room)
        slow = jnp.logical_not(fast)
        rel = ki - (q_off // sw + qi * n_strips)
        pl.when(fast & (rel < 0))(lambda: fast_step(["full"] * n_strips))
        for r in range(n_strips):
            modes = ["skip"] * r + ["masked"] + ["full"] * (n_strips - r - 1)
            pl.when(fast & (rel == r))(functools.partial(fast_step, modes))
        pl.when(slow)(lambda: step(["masked"] * n_strips))
    else:
        step(["masked"] * n_strips)

    @pl.when(last_ref[p_id] == 1)
    def _():
        inv = 1.0 / l_ref[...]
        parts = [acc_ref[h * V_HEAD:(h + 1) * V_HEAD, :] * inv[h:h + 1, :] for h in range(MLA_HEADS)]
        o_ref[0] = jnp.concatenate(parts, axis=0).T.astype(BF16)


def _attn(qt, k, vt, qn2, kn2, tq, tk, sw, q_off):
    B, nqf, Tq = qt.shape
    Tk = k.shape[1]
    nq, nk = Tq // tq, Tk // tk
    nt_a = qn2.shape[1]
    norm_tiles = (max(1, nt_a // nq), max(1, nt_a // nk), nt_a)
    qi_l, ki_l, last_l = [], [], []
    for qi in range(nq):
        last_chunk = (q_off + qi * tq + tq - 1) // CHUNK
        kmax = min(nk - 1, ((last_chunk + 1) * CHUNK - 1) // tk)
        for ki in range(kmax + 1):
            qi_l.append(qi), ki_l.append(ki), last_l.append(int(ki == kmax))
    tabs = [jnp.asarray(np.asarray(a, np.int32)) for a in (qi_l, ki_l, last_l)]
    tabs += [qn2.reshape(-1), kn2.reshape(-1)]
    grid_spec = pltpu.PrefetchScalarGridSpec(
        num_scalar_prefetch=5, grid=(B, len(qi_l)),
        in_specs=[pl.BlockSpec((1, nqf, tq), lambda b, p, qi, ki, *_: (b, 0, qi[p])),
                  pl.BlockSpec((1, tk, MLA_HEADS * HEAD_PAD), lambda b, p, qi, ki, *_: (b, ki[p], 0)),
                  pl.BlockSpec((1, MLA_HEADS * V_HEAD, tk), lambda b, p, qi, ki, *_: (b, 0, ki[p]))],
        out_specs=pl.BlockSpec((1, tq, MLA_HEADS * V_HEAD), lambda b, p, qi, ki, *_: (b, qi[p], 0)),
        scratch_shapes=[pltpu.VMEM((MLA_HEADS, tq), F32), pltpu.VMEM((MLA_HEADS, tq), F32),
                        pltpu.VMEM((MLA_HEADS * V_HEAD, tq), F32), pltpu.VMEM((2, tk, sw), F32),
                        pltpu.VMEM((2, tk, sw), BF16), pltpu.SMEM((MLA_HEADS,), F32)])
    return pl.pallas_call(
        functools.partial(_attn_kernel, tq, tk, sw, q_off, norm_tiles), grid_spec=grid_spec,
        out_shape=jax.ShapeDtypeStruct((B, Tq, MLA_HEADS * V_HEAD), BF16),
        compiler_params=pltpu.CompilerParams(dimension_semantics=("parallel", "arbitrary"),
                                             vmem_limit_bytes=VMEM_LIMIT),
        name="attn",
    )(*tabs, qt, k, vt)


def _gla_kernel(lc, n_chunks, nch, gq_ref, gk_ref, lg_ref, gv_ref, gg_ref, gon_ref, s0_ref, og_ref, sfin_ref,
                st_ref):
    n_sub = lc // SUB
    n_stack = SUB * n_sub * (n_sub - 1) // 2
    stack_pad = -(-n_stack // LANES) * LANES

    @pl.when(pl.program_id(1) == 0)
    def _():
        st_ref[...] = s0_ref[0]

    lane_k = lax.broadcasted_iota(jnp.int32, (lc, GLA_K), 1) // GLA_DK
    ri = lax.broadcasted_iota(jnp.int32, (lc, lc), 0)
    ci = lax.broadcasted_iota(jnp.int32, (lc, lc), 1)
    tri = (ci <= ri).astype(BF16)
    e_r = lax.broadcasted_iota(jnp.int32, (GLA_K, GLA_V), 0) // GLA_DK
    e_c = lax.broadcasted_iota(jnp.int32, (GLA_K, GLA_V), 1) // GLA_DV
    e_mat = (e_r == e_c).astype(BF16)
    sub_row = lax.broadcasted_iota(jnp.int32, (SUB, GLA_K), 0)
    st_lane = lax.broadcasted_iota(jnp.int32, (GLA_DV, GLA_K), 1) // GLA_DK
    a_row = lax.broadcasted_iota(jnp.int32, (lc, stack_pad), 0) // SUB
    a_col = lax.broadcasted_iota(jnp.int32, (lc, stack_pad), 1)
    off_mask = jnp.zeros((lc, stack_pad), jnp.bool_)
    seg = 0
    for i_sub in range(1, n_sub):
        off_mask = off_mask | ((a_row == i_sub) & (a_col >= seg) & (a_col < seg + SUB * i_sub))
        seg += SUB * i_sub

    def heads_on_rows(a):
        return jnp.concatenate([jnp.where(lane_k == h, a, 0.0) for h in range(GLA_HEADS)], axis=0).astype(BF16)

    def load(c):
        r = pl.ds(pl.multiple_of(c * lc, lc), lc)
        lg = lg_ref[0, r, :]
        hi = lg.astype(BF16)
        r1 = lg - hi.astype(F32)
        mid = r1.astype(BF16)
        lo = (r1 - mid.astype(F32)).astype(BF16)
        b = _dot(tri, hi) + _dot(tri, mid) + _dot(tri, lo)
        return dict(r=r, q=gq_ref[0, r, :], k=gk_ref[0, r, :], v=gv_ref[0, r, :].astype(F32), b=b)

    def local(d):
        q, k, v, b = d["q"], d["k"], d["v"], d["b"]
        v_b = v.astype(BF16)
        b_last = b[lc - 1:lc, :]
        d["decay"] = jnp.exp(b_last)
        d["q4"] = heads_on_rows(q * jnp.exp(b))
        kd = (k * jnp.exp(b_last - b)).astype(BF16)
        full = _tn(v_b, kd)
        upd = jnp.zeros((GLA_DV, GLA_K), F32)
        for h in range(GLA_HEADS):
            upd = upd + jnp.where(st_lane == h, full[h * GLA_DV:(h + 1) * GLA_DV], 0.0)
        d["upd"] = upd
        refs = [jnp.zeros((SUB, GLA_K), F32)]
        kst, vst = [], []
        for i_sub in range(1, n_sub):
            r_i = b[SUB * i_sub - 1:SUB * i_sub, :]
            refs.append(jnp.broadcast_to(r_i, (SUB, GLA_K)))
            kst.append(k[:SUB * i_sub] * jnp.exp(r_i - b[:SUB * i_sub]))
            vst.append(v[:SUB * i_sub])
        if stack_pad > n_stack:
            kst.append(jnp.zeros((stack_pad - n_stack, GLA_K), F32))
            vst.append(jnp.zeros((stack_pad - n_stack, GLA_V), F32))
        kst = jnp.concatenate(kst, axis=0).astype(BF16)
        vst = jnp.concatenate(vst, axis=0).astype(BF16)
        qp = q * jnp.exp(b - jnp.concatenate(refs, axis=0))
        a4 = _nt(heads_on_rows(qp), kst)
        d["o_off"] = [
            _dot(jnp.where(off_mask, a4[h * lc:(h + 1) * lc], 0.0).astype(BF16), vst[:, h * GLA_DV:(h + 1) * GLA_DV])
            for h in range(GLA_HEADS)]

    def same_sub(d):
        q, k, v, b = d["q"], d["k"], d["v"], d["b"]
        terms = []
        for i_sub in range(n_sub):
            rs = slice(SUB * i_sub, SUB * (i_sub + 1))
            q_i, k_i, b_i = q[rs], k[rs], b[rs]
            for j in range(SUB):
                arg = jnp.where(sub_row >= j, b_i - b_i[j:j + 1], NEG)
                terms.append(q_i * (k_i[j:j + 1] * jnp.exp(arg)))
        w = _dot(jnp.concatenate(terms, axis=0).astype(BF16), e_mat)
        o_diag = []
        for i_sub in range(n_sub):
            acc = None
            for j in range(SUB):
                row = SUB * i_sub + j
                t = w[row * SUB:(row + 1) * SUB] * v[row:row + 1]
                acc = t if acc is None else acc + t
            o_diag.append(acc)
        d["o_diag"] = jnp.concatenate(o_diag, axis=0)

    def carried(d):
        st = st_ref[...]
        d["o_heads"] = _nt(d["q4"], st.astype(BF16))
        st_ref[...] = d["decay"] * st + d["upd"]

    def finish(d):
        o = jnp.concatenate(
            [d["o_heads"][h * lc:(h + 1) * lc] + d["o_off"][h] for h in range(GLA_HEADS)], axis=1) + d["o_diag"]
        gon = gon_ref[...]
        normed = jnp.concatenate(
            [_rms(o[:, h * GLA_DV:(h + 1) * GLA_DV], gon) for h in range(GLA_HEADS)], axis=1)
        gg = gg_ref[0, d["r"], :]
        og_ref[0, d["r"], :] = (normed * (gg * jax.nn.sigmoid(gg))).astype(BF16)

    def body(it, carry):
        ds = [load(it * nch + u) for u in range(nch)]
        for stage in (local, same_sub, carried, finish):
            for d in ds:
                stage(d)
        return carry

    lax.fori_loop(0, n_chunks // nch, body, 0)

    @pl.when(pl.program_id(1) == pl.num_programs(1) - 1)
    def _():
        sfin_ref[0] = st_ref[...]


def _gla(gq, gk, lg, gv, gg, g_on, s0t, lc, tg):
    B, T, _ = gq.shape
    ng = T // tg
    tok = lambda w: pl.BlockSpec((1, tg, w), lambda b, g: (b, g, 0))
    st_spec = pl.BlockSpec((1, GLA_DV, GLA_K), lambda b, g: (b, 0, 0))
    S = jax.ShapeDtypeStruct
    return pl.pallas_call(
        functools.partial(_gla_kernel, lc, tg // lc, _pick(tg // lc, (4, 2))), grid=(B, ng),
        in_specs=[tok(GLA_K), tok(GLA_K), tok(GLA_K), tok(GLA_V), tok(GLA_V),
                  pl.BlockSpec((1, GLA_DV), lambda b, g: (0, 0)), st_spec],
        out_specs=[tok(GLA_V), st_spec],
        out_shape=[S((B, T, GLA_V), BF16), S((B, GLA_DV, GLA_K), F32)],
        scratch_shapes=[pltpu.VMEM((GLA_DV, GLA_K), F32)],
        compiler_params=pltpu.CompilerParams(dimension_semantics=("parallel", "arbitrary"),
                                             vmem_limit_bytes=VMEM_LIMIT),
        name="gla",
    )(gq, gk, lg, gv, gg, g_on, s0t)


def _ffn_kernel(final, x_ref, om_ref, og_ref, wo_ref, gf_ref, wg_ref, wu_ref, wd_ref, gfin_ref, y_ref):
    cat = jnp.concatenate([om_ref[0], og_ref[0]], axis=1)
    x1 = x_ref[0] + _dot(cat, wo_ref[...])
    hn = _rms(x1, gf_ref[...]).astype(BF16)
    a = _dot(hn, wg_ref[...])
    u = _dot(hn, wu_ref[...])
    act = (a * jax.nn.sigmoid(a) * u).astype(BF16)
    x2 = x1 + _dot(act, wd_ref[...])
    y_ref[0] = _rms(x2, gfin_ref[...]) if final else x2


def _ffn(x, o_mla, o_g, lw, g_final, tm, final):
    B, T, D = x.shape
    tok = lambda w: pl.BlockSpec((1, tm, w), lambda b, t: (b, t, 0))
    const = lambda a: pl.BlockSpec(a.shape, lambda b, t: (0,) * a.ndim, pipeline_mode=pl.Buffered(1))
    d_ff = lw["w_down"].shape[0]
    half = lambda j: pl.BlockSpec((D, d_ff), lambda b, t: (0, j), pipeline_mode=pl.Buffered(1))
    weights = (lw["w_o"], lw["g_ffn"], lw["w_gu"], lw["w_gu"], lw["w_down"], g_final)
    w_specs = [const(lw["w_o"]), const(lw["g_ffn"]), half(0), half(1), const(lw["w_down"]), const(g_final)]
    return pl.pallas_call(
        functools.partial(_ffn_kernel, final), grid=(B, T // tm),
        in_specs=[tok(D), tok(o_mla.shape[-1]), tok(o_g.shape[-1])] + w_specs,
        out_specs=tok(D), out_shape=jax.ShapeDtypeStruct((B, T, D), F32),
        compiler_params=pltpu.CompilerParams(dimension_semantics=("parallel", "parallel"),
                                             vmem_limit_bytes=VMEM_LIMIT),
        name="ffn_final" if final else "ffn",
    )(x, o_mla, o_g, *weights)


def _layer_weights(l, g_attn, w_in, g_qn, w_uq, g_kvn, w_ukv, w_a2, b_a2, g_gla_on, w_o, g_ffn, w_gu, w_down):
    d_model = w_in.shape[1]
    wi = w_in[l]
    c = np.cumsum([0, Q_LORA, KV_LORA, QK_ROPE, GLA_K, GLA_K, GLA_V, GLA_V, GATE_LR])
    sm = jnp.concatenate([wi[:, c[2]:c[3]], wi[:, c[7]:c[8]],
                          jnp.zeros((d_model, LANES - QK_ROPE - GATE_LR), F32)], axis=1)
    w_in_r = jnp.concatenate([wi[:, c[0]:c[2]], wi[:, c[3]:c[4]] * (GLA_DK ** -0.5), wi[:, c[4]:c[7]], sm], axis=1)
    wq = w_uq[l].reshape(Q_LORA, MLA_HEADS, QK_NOPE + QK_ROPE)
    w_qn_t = wq[:, :, :QK_NOPE].reshape(Q_LORA, -1).T
    w_qr_t = jnp.concatenate([wq[:, :, QK_NOPE:QK_NOPE + ROPE_HALF].reshape(Q_LORA, -1),
                              wq[:, :, QK_NOPE + ROPE_HALF:].reshape(Q_LORA, -1)], axis=1).T
    wkv = w_ukv[l].reshape(KV_LORA, MLA_HEADS, QK_NOPE + V_HEAD)
    wk_nope = jnp.pad(wkv[:, :, :QK_NOPE], ((0, 0), (0, 0), (0, HEAD_PAD - QK_NOPE))).reshape(KV_LORA, -1)
    place = np.zeros((LANES, MLA_HEADS, HEAD_PAD), np.float32)
    for d in range(QK_ROPE):
        place[d, :, QK_NOPE + d] = 1.0
    w_k = jnp.concatenate([wk_nope, jnp.asarray(place.reshape(LANES, -1))], axis=0)
    w_v_t = wkv[:, :, QK_NOPE:].reshape(KV_LORA, -1).T
    w_uk_t = wkv[:, :, :QK_NOPE].transpose(1, 2, 0)
    w_uv_pad = jnp.zeros((MLA_HEADS, KV_LORA, MLA_HEADS * V_HEAD), F32)
    for hd in range(MLA_HEADS):
        w_uv_pad = w_uv_pad.at[hd, :, hd * V_HEAD:(hd + 1) * V_HEAD].set(wkv[:, hd, QK_NOPE:])
    w_a2_p = jnp.zeros((LANES, GLA_K), F32).at[QK_ROPE:QK_ROPE + GATE_LR].set(w_a2[l])
    row = lambda v: v.reshape(1, -1)
    n_nope, n_rope = MLA_HEADS * QK_NOPE, MLA_HEADS * ROPE_HALF
    e_q = np.zeros((BF16_ROWS, n_nope + 2 * n_rope), np.float32)
    e_k = np.zeros((MLA_HEADS * HEAD_PAD, LANES), np.float32)
    for h in range(MLA_HEADS):
        e_q[h, h * QK_NOPE:(h + 1) * QK_NOPE] = 1.0
        e_q[h, n_nope + h * ROPE_HALF:n_nope + (h + 1) * ROPE_HALF] = 1.0
        e_q[h, n_nope + n_rope + h * ROPE_HALF:n_nope + n_rope + (h + 1) * ROPE_HALF] = 1.0
        e_k[h * HEAD_PAD:(h + 1) * HEAD_PAD, h] = 1.0
    return {
        "e_q": jnp.asarray(e_q, BF16), "e_k": jnp.asarray(e_k, BF16),
        "g_attn": row(g_attn[l]), "w_in": w_in_r.astype(BF16),
        "g_qn": row(g_qn[l] * (ATTN_SCALE * LOG2E)), "w_qnT": w_qn_t.astype(BF16), "w_qrT": w_qr_t.astype(BF16),
        "g_kvn": row(g_kvn[l]), "w_k": w_k.astype(BF16), "w_vT": w_v_t.astype(BF16),
        "w_ukT": w_uk_t.astype(BF16), "w_uv_pad": w_uv_pad.astype(BF16),
        "w_a2": w_a2_p.astype(BF16), "b_a2": row(b_a2[l]),
        "g_on": row(g_gla_on[l]), "w_o": w_o[l].astype(BF16), "g_ffn": row(g_ffn[l]),
        "w_gu": w_gu[l].astype(BF16), "w_down": w_down[l].astype(BF16),
    }


def _rope_tables(pos):
    inv = ROPE_THETA ** (-jnp.arange(ROPE_HALF, dtype=F32) / ROPE_HALF)
    ang = pos.astype(F32)[:, None] * inv[None, :]
    cos, sin = jnp.cos(ang), jnp.sin(ang)
    zeros = jnp.zeros((pos.shape[0], LANES - QK_ROPE), F32)
    cr = jnp.concatenate([cos, cos, zeros], axis=1)
    sr = jnp.concatenate([-sin, sin, zeros], axis=1)
    ct = jnp.tile(cos, (1, MLA_HEADS)).T
    st = jnp.tile(sin, (1, MLA_HEADS)).T
    return cr, sr, ct, st


def _pick(n, pref):
    for t in pref:
        if n % t == 0:
            return t
    return n


def kernel(x_prompt, x_sample, cache_ckv, cache_krope, state_gla, g_attn, w_in, g_qn, w_uq, g_kvn, w_ukv, w_a2, b_a2,
           g_gla_on, w_o, g_ffn, w_gu, w_down, g_final):
    depth = w_in.shape[0]
    B, T, D = x_prompt.shape
    nb, ts, _ = x_sample.shape
    past = cache_ckv.shape[2]
    assert T % CHUNK == 0 and ts % BF16_ROWS == 0 and ts <= CHUNK

    tabs_p = _rope_tables(jnp.arange(T, dtype=jnp.int32))
    tabs_s = _rope_tables(jnp.tile(past + jnp.arange(ts, dtype=jnp.int32), nb))
    tm_p = _pick(T, (512, 256, 128))
    tk = _pick(T, (ATTN_TK, 256, 128, 64))
    tq = ATTN_TQ if (T % ATTN_TQ == 0 and ATTN_TQ % tk == 0) else tk
    tg = _pick(T, (512, 256, 128, 64))
    g_fin = g_final.reshape(1, -1)

    xp = x_prompt
    xs = x_sample.reshape(1, nb * ts, D)
    outs = {k: [] for k in ("ckv_p", "kr_p", "st_p", "ckv_s", "kr_s", "st_s")}
    for l in range(depth):
        lw = _layer_weights(l, g_attn, w_in, g_qn, w_uq, g_kvn, w_ukv, w_a2, b_a2, g_gla_on, w_o, g_ffn, w_gu,
                            w_down)
        final = l == depth - 1
        ckv, kr, qt, k, vt, qn2, kn2, gq, gk, gv, gg, lg = _proj(xp, tabs_p, lw, tm_p, True)
        o_mla = _attn(qt, k, vt, qn2[:, :, :MLA_HEADS, 0], kn2[:, :, 0, :MLA_HEADS], tq, tk, tk, 0)
        s0 = jnp.zeros((B, GLA_DV, GLA_K), F32)
        o_g, st_fin = _gla(gq, gk, lg, gv, gg, lw["g_on"], s0, CHUNK, tg)
        xp = _ffn(xp, o_mla, o_g, lw, g_fin, tm_p, final)
        outs["ckv_p"].append(ckv), outs["kr_p"].append(kr)
        outs["st_p"].append(st_fin.reshape(B, GLA_DV, GLA_HEADS, GLA_DK).transpose(0, 2, 3, 1))
        ckv, kr, qt, gq, gk, gv, gg, lg = _proj(xs, tabs_s, lw, nb * ts, False)
        ckv, kr = ckv.reshape(nb, ts, -1), kr.reshape(nb, ts, -1)
        n_nope, n_rope = MLA_HEADS * QK_NOPE, MLA_HEADS * ROPE_HALF
        per_head = lambda a: a.reshape(MLA_HEADS, -1, nb * ts)
        qt = jnp.concatenate([per_head(qt[0, :n_nope]), per_head(qt[0, n_nope:n_nope + n_rope]),
                              per_head(qt[0, n_nope + n_rope:]),
                              jnp.zeros((MLA_HEADS, HEAD_PAD - QK_NOPE - QK_ROPE, nb * ts), BF16)], axis=1)
        qt = qt.reshape(MLA_HEADS, HEAD_PAD, nb, ts).transpose(2, 0, 1, 3)
        o_mla = _cache_attn(cache_ckv[l], cache_krope[l], ckv, kr, qt, lw).reshape(1, nb * ts, -1)
        s0 = state_gla[l].astype(F32).transpose(0, 3, 1, 2).reshape(nb, GLA_DV, GLA_K)
        per_b = lambda a: a.reshape(nb, ts, -1)
        o_g, st_fin = _gla(per_b(gq), per_b(gk), per_b(lg), per_b(gv), per_b(gg), lw["g_on"], s0, ts, ts)
        xs = _ffn(xs, o_mla, o_g.reshape(1, nb * ts, -1), lw, g_fin, nb * ts, final)
        outs["ckv_s"].append(ckv), outs["kr_s"].append(kr)
        outs["st_s"].append(st_fin.reshape(nb, GLA_DV, GLA_HEADS, GLA_DK).transpose(0, 2, 3, 1))

    stack = lambda key: jnp.stack(outs[key])
    return (xp, xs.reshape(nb, ts, D), stack("ckv_p"), stack("kr_p"), stack("st_p"),
            stack("ckv_s"), stack("kr_s"), stack("st_s"))
```

```python
import functools

import jax
import jax.numpy as jnp
import numpy as np
from jax import lax
from jax.experimental import pallas as pl
from jax.experimental.pallas import tpu as pltpu

F32 = jnp.float32
BF16 = jnp.bfloat16

CHUNK = 64
EPS = 1e-6
MLA_HEADS = 8
QK_NOPE = 64
QK_ROPE = 32
ROPE_HALF = QK_ROPE // 2
V_HEAD = 64
Q_LORA = 256
KV_LORA = 256
ROPE_THETA = 10000.0
ATTN_SCALE = (QK_NOPE + QK_ROPE) ** -0.5
LOG2E = 1.4426950408889634
GLA_HEADS = 4
GLA_DK = 64
GLA_DV = 128
GATE_LR = 16
GATE_TAU = 16.0
GLA_K = GLA_HEADS * GLA_DK
GLA_V = GLA_HEADS * GLA_DV

LANES = 128
BF16_ROWS = 16
HEAD_PAD = 128
PV_ROWS = 256
ATTN_TK = 512
ATTN_TQ = 1024
SUB = 8
NEG = -1e30
EXP_RANGE = 64.0
NORM_MARGIN = 1.05
VMEM_LIMIT = 48 * 1024 * 1024

Z_CQ = 0
Z_CKV = Z_CQ + Q_LORA
Z_GQ = Z_CKV + KV_LORA
Z_GK = Z_GQ + GLA_K
Z_GV = Z_GK + GLA_K
Z_GG = Z_GV + GLA_V
Z_SM = Z_GG + GLA_V
Z_W = Z_SM + LANES


def _nt(a, b):
    return lax.dot_general(a, b, (((1,), (1,)), ((), ())), preferred_element_type=F32)


def _tn(a, b):
    return lax.dot_general(a, b, (((0,), (0,)), ((), ())), preferred_element_type=F32)


def _dot(a, b):
    return jnp.dot(a, b, preferred_element_type=F32)


def _rms(x, g):
    return x * lax.rsqrt(jnp.mean(x * x, axis=-1, keepdims=True) + EPS) * g


def _proj_kernel(emit_kv, x_ref, cro_ref, sro_ref, crb_ref, srb_ref, cto_ref, sto_ref, ctb_ref, stb_ref, gat_ref,
                 win_ref, gqn_ref, wqn_ref, wqr_ref, gkv_ref, wk_ref, wvt_ref, wa2_ref, ba2_ref, eq_ref, ek_ref, *outs):
    if emit_kv:
        ckv_ref, kr_ref, qt_ref, k_ref, vt_ref, qn2_ref, kn2_ref, gq_ref, gk_ref, gv_ref, gg_ref, lg_ref = outs
    else:
        ckv_ref, kr_ref, qt_ref, gq_ref, gk_ref, gv_ref, gg_ref, lg_ref = outs
    x = x_ref[0]
    h = _rms(x, gat_ref[...]).astype(BF16)
    z = _dot(h, win_ref[...])

    cqn = _rms(z[:, Z_CQ:Z_CQ + Q_LORA], gqn_ref[...]).astype(BF16)
    n_nope = MLA_HEADS * QK_NOPE
    n_rope = MLA_HEADS * ROPE_HALF
    q_nope = _nt(wqn_ref[...], cqn)
    qt_ref[0, 0:n_nope, :] = q_nope.astype(BF16)
    qr = _nt(wqr_ref[...], cqn)
    x1, x2 = qr[:n_rope], qr[n_rope:]
    tm = qr.shape[1]
    cb, sb = jnp.broadcast_to(ctb_ref[0, :, 0:1], (n_rope, tm)), jnp.broadcast_to(stb_ref[0, :, 0:1], (n_rope, tm))
    ct = cto_ref[...] * cb - sto_ref[...] * sb
    st = sto_ref[...] * cb + cto_ref[...] * sb
    q_r1, q_r2 = x1 * ct - x2 * st, x1 * st + x2 * ct
    qt_ref[0, n_nope:n_nope + n_rope, :] = q_r1.astype(BF16)
    qt_ref[0, n_nope + n_rope:, :] = q_r2.astype(BF16)

    ckv = _rms(z[:, Z_CKV:Z_CKV + KV_LORA], gkv_ref[...])
    ckv_ref[0] = ckv
    u = z[:, Z_SM:Z_SM + LANES]
    lane = lax.broadcasted_iota(jnp.int32, u.shape, 1)
    partner = jnp.where(lane < ROPE_HALF, pltpu.roll(u, LANES - ROPE_HALF, 1), pltpu.roll(u, ROPE_HALF, 1))
    cos_r = cro_ref[...] * crb_ref[0] - sro_ref[...] * srb_ref[0]
    sin_r = sro_ref[...] * crb_ref[0] + cro_ref[...] * srb_ref[0]
    kr = u * cos_r + partner * jnp.where(lane < ROPE_HALF, -sin_r, sin_r)
    kr_ref[0] = kr[:, :QK_ROPE]
    if emit_kv:
        ckv_b = ckv.astype(BF16)
        k_all = _dot(jnp.concatenate([ckv_b, kr.astype(BF16)], axis=1), wk_ref[...])
        k_ref[0] = k_all.astype(BF16)
        vt_ref[0] = _nt(wvt_ref[...], ckv_b).astype(BF16)
        q_sq = jnp.concatenate([q_nope * q_nope, q_r1 * q_r1, q_r2 * q_r2], axis=0).astype(BF16)
        qn2 = jnp.max(_dot(eq_ref[...], q_sq), axis=1, keepdims=True)
        qn2_ref[0, 0] = jnp.broadcast_to(qn2, qn2_ref.shape[2:])
        kn2 = _dot((k_all * k_all).astype(BF16), ek_ref[...])
        kn2_ref[0, 0] = jnp.max(kn2, axis=0, keepdims=True)

    gq_ref[0] = z[:, Z_GQ:Z_GQ + GLA_K]
    gk_ref[0] = z[:, Z_GK:Z_GK + GLA_K]
    gv_ref[0] = z[:, Z_GV:Z_GV + GLA_V].astype(BF16)
    gg_ref[0] = z[:, Z_GG:Z_GG + GLA_V]
    xg = _dot(u.astype(BF16), wa2_ref[...]) + ba2_ref[...]
    lg_ref[0] = (jnp.minimum(xg, 0.0) - jnp.log1p(jnp.exp(-jnp.abs(xg)))) * (1.0 / GATE_TAU)


def _proj(x, tabs, lw, tm, emit_kv):
    B, T, D = x.shape
    nt = T // tm
    tok = lambda w: pl.BlockSpec((1, tm, w), lambda b, t: (b, t, 0))
    feat = lambda r: pl.BlockSpec((1, r, tm), lambda b, t: (b, 0, t))
    full = lambda a: pl.BlockSpec(a.shape, lambda b, t: (0,) * a.ndim)
    weights = (lw["g_attn"], lw["w_in"], lw["g_qn"], lw["w_qnT"], lw["w_qrT"], lw["g_kvn"], lw["w_k"], lw["w_vT"],
               lw["w_a2"], lw["b_a2"], lw["e_q"], lw["e_k"])
    off_r = pl.BlockSpec((tm, LANES), lambda b, t: (0, 0))
    base_r = pl.BlockSpec((1, 1, LANES), lambda b, t: (t, 0, 0))
    off_t = pl.BlockSpec((LANES, tm), lambda b, t: (0, 0))
    base_t = pl.BlockSpec((1, LANES, LANES), lambda b, t: (t, 0, 0))
    in_specs = [tok(D), off_r, off_r, base_r, base_r, off_t, off_t, base_t, base_t]
    in_specs += [full(w) for w in weights]
    S = jax.ShapeDtypeStruct
    nq = MLA_HEADS * (QK_NOPE + QK_ROPE)
    out_shape = [S((B, T, KV_LORA), F32), S((B, T, QK_ROPE), F32), S((B, nq, T), BF16)]
    out_specs = [tok(KV_LORA), tok(QK_ROPE), feat(nq)]
    if emit_kv:
        out_shape += [S((B, T, MLA_HEADS * HEAD_PAD), BF16), S((B, MLA_HEADS * V_HEAD, T), BF16),
                      S((B, nt, BF16_ROWS, LANES), F32), S((B, nt, 1, LANES), F32)]
        out_specs += [tok(MLA_HEADS * HEAD_PAD), feat(MLA_HEADS * V_HEAD),
                      pl.BlockSpec((1, 1, BF16_ROWS, LANES), lambda b, t: (b, t, 0, 0)),
                      pl.BlockSpec((1, 1, 1, LANES), lambda b, t: (b, t, 0, 0))]
    out_shape += [S((B, T, GLA_K), F32), S((B, T, GLA_K), F32), S((B, T, GLA_V), BF16), S((B, T, GLA_V), F32),
                  S((B, T, GLA_K), F32)]
    out_specs += [tok(GLA_K), tok(GLA_K), tok(GLA_V), tok(GLA_V), tok(GLA_K)]
    return pl.pallas_call(
        functools.partial(_proj_kernel, emit_kv),
        grid=(B, nt), in_specs=in_specs, out_specs=out_specs, out_shape=out_shape,
        compiler_params=pltpu.CompilerParams(dimension_semantics=("parallel", "parallel"),
                                             vmem_limit_bytes=VMEM_LIMIT),
        name="proj_kv" if emit_kv else "proj",
    )(x, *tabs, *weights)


def _cache_attn_kernel(past, cckv_ref, ckr_ref, nckv_ref, nkr_ref, qt_ref, wuk_ref, wuv_ref, o_ref):
    ts = qt_ref.shape[-1]
    ckv = jnp.concatenate([cckv_ref[0], nckv_ref[0]], axis=0).astype(BF16)
    kr = jnp.concatenate([ckr_ref[0], nkr_ref[0]], axis=0).astype(BF16)
    eye = (lax.broadcasted_iota(jnp.int32, (QK_ROPE, QK_ROPE), 0)
           == lax.broadcasted_iota(jnp.int32, (QK_ROPE, QK_ROPE), 1)).astype(BF16)
    q_lat, q_rope = [], []
    for h in range(MLA_HEADS):
        q_lat.append(_tn(qt_ref[0, h, 0:QK_NOPE, :], wuk_ref[h]).astype(BF16))
        q_rope.append(_tn(qt_ref[0, h, QK_NOPE:QK_NOPE + QK_ROPE, :], eye).astype(BF16))
    s = _nt(jnp.concatenate(q_lat, axis=0), ckv) + _nt(jnp.concatenate(q_rope, axis=0), kr)
    kpos = lax.broadcasted_iota(jnp.int32, s.shape, 1)
    qpos = past + lax.broadcasted_iota(jnp.int32, s.shape, 0) % ts
    s = jnp.where((kpos // CHUNK) <= (qpos // CHUNK), s, NEG)
    p = jnp.exp2(s - jnp.max(s, axis=1, keepdims=True))
    o_lat = _dot(p.astype(BF16), ckv) / jnp.sum(p, axis=1, keepdims=True)
    o = _dot(o_lat[0:ts].astype(BF16), wuv_ref[0])
    for h in range(1, MLA_HEADS):
        o = o + _dot(o_lat[h * ts:(h + 1) * ts].astype(BF16), wuv_ref[h])
    o_ref[0] = o.astype(BF16)


def _cache_attn(l, cache_ckv, cache_kr, new_ckv, new_kr, qt, lw):
    _, nb, past, _ = cache_ckv.shape
    ts = new_ckv.shape[1]
    per_b = lambda a: pl.BlockSpec((1,) + a.shape[1:], lambda b: (b,) + (0,) * (a.ndim - 1))
    layer_b = lambda a: pl.BlockSpec((None, 1) + a.shape[2:], lambda b: (l, b, 0, 0))
    full = lambda a: pl.BlockSpec(a.shape, lambda b: (0,) * a.ndim)
    return pl.pallas_call(
        functools.partial(_cache_attn_kernel, past), grid=(nb,),
        in_specs=[layer_b(cache_ckv), layer_b(cache_kr), per_b(new_ckv), per_b(new_kr), per_b(qt), full(lw["w_ukT"]),
                  full(lw["w_uv_pad"])],
        out_specs=pl.BlockSpec((1, ts, MLA_HEADS * V_HEAD), lambda b: (b, 0, 0)),
        out_shape=jax.ShapeDtypeStruct((nb, ts, MLA_HEADS * V_HEAD), BF16),
        compiler_params=pltpu.CompilerParams(dimension_semantics=("parallel",), vmem_limit_bytes=VMEM_LIMIT),
        name="cache_attn",
    )(cache_ckv, cache_kr, new_ckv, new_kr, qt, lw["w_ukT"], lw["w_uv_pad"])


def _attn_kernel(tq, tk, sw, q_off, norm_tiles, qi_ref, ki_ref, last_ref, qn2_ref, kn2_ref, qt_ref, k_ref, vt_ref,
                 o_ref, m_ref, l_ref, acc_ref, s_ref, p_ref, mmin_ref):
    nq_a, nk_a, nt_a = norm_tiles
    p_id = pl.program_id(1)
    qi, ki = qi_ref[p_id], ki_ref[p_id]
    n_nope = MLA_HEADS * QK_NOPE
    n_rope = MLA_HEADS * ROPE_HALF

    @pl.when(ki == 0)
    def _():
        m_ref[...] = jnp.full(m_ref.shape, NEG, F32)
        l_ref[...] = jnp.zeros(l_ref.shape, F32)
        acc_ref[...] = jnp.zeros(acc_ref.shape, F32)

    rb = PV_ROWS if tk % PV_ROWS == 0 else tk
    ones_rows = jnp.ones((BF16_ROWS, rb), BF16)

    n_strips = tq // sw

    def qk(h, i, masked):
        cols = slice(i * sw, (i + 1) * sw)
        qt_h = jnp.concatenate(
            [qt_ref[0, h * QK_NOPE:(h + 1) * QK_NOPE, cols],
             qt_ref[0, n_nope + h * ROPE_HALF:n_nope + (h + 1) * ROPE_HALF, cols],
             qt_ref[0, n_nope + n_rope + h * ROPE_HALF:n_nope + n_rope + (h + 1) * ROPE_HALF, cols],
             jnp.zeros((HEAD_PAD - QK_NOPE - QK_ROPE, sw), BF16)], axis=0)
        s = _dot(k_ref[0, :, h * HEAD_PAD:(h + 1) * HEAD_PAD], qt_h)
        if masked:
            kpos = ki * tk + lax.broadcasted_iota(jnp.int32, (tk, sw), 0)
            qpos = q_off + qi * tq + i * sw + lax.broadcasted_iota(jnp.int32, (tk, sw), 1)
            s = jnp.where((kpos // CHUNK) <= (qpos // CHUNK), s, NEG)
        return s

    def pv_rows(h, r):
        return jnp.concatenate([vt_ref[0, h * V_HEAD:(h + 1) * V_HEAD, r * rb:(r + 1) * rb], ones_rows], axis=0)

    def fast_step(modes):
        stages = [(h, i) for h in range(MLA_HEADS) for i in range(n_strips) if modes[i] != "skip"]

        def weights(t):
            h, i = stages[t]
            s = qk(h, i, modes[i] == "masked")
            p_ref[t % 2] = jnp.exp2(s - m_ref[h:h + 1, i * sw:(i + 1) * sw]).astype(BF16)

        weights(0)
        for t, (h, i) in enumerate(stages):
            if t + 1 < len(stages):
                weights(t + 1)
            pv = None
            for r in range(tk // rb):
                d = _dot(pv_rows(h, r), p_ref[t % 2, r * rb:(r + 1) * rb, :])
                pv = d if pv is None else pv + d
            rows, cols = slice(h * V_HEAD, (h + 1) * V_HEAD), slice(i * sw, (i + 1) * sw)
            acc_ref[rows, cols] = acc_ref[rows, cols] + pv[:V_HEAD]
            l_ref[h:h + 1, cols] = l_ref[h:h + 1, cols] + pv[V_HEAD:V_HEAD + 1]

    def step(modes):
        stages = [(h, i) for h in range(MLA_HEADS) for i in range(n_strips) if modes[i] != "skip"]

        def scores(t):
            h, i = stages[t]
            s = qk(h, i, modes[i] == "masked")
            s_ref[t % 2] = s
            return jnp.max(s, axis=0, keepdims=True)

        smax_next = scores(0)
        for t, (h, i) in enumerate(stages):
            smax = smax_next
            if t + 1 < len(stages):
                smax_next = scores(t + 1)
            cols = slice(i * sw, (i + 1) * sw)
            m_old = m_ref[h:h + 1, cols]
            m_new = jnp.maximum(m_old, smax)
            alpha = jnp.exp2(m_old - m_new)
            m_ref[h:h + 1, cols] = m_new
            rows = slice(h * V_HEAD, (h + 1) * V_HEAD)
            pv = None
            for r in range(tk // rb):
                p = jnp.exp2(s_ref[t % 2, r * rb:(r + 1) * rb, :] - m_new).astype(BF16)
                d = _dot(pv_rows(h, r), p)
                pv = d if pv is None else pv + d
            acc_ref[rows, cols] = alpha * acc_ref[rows, cols] + pv[:V_HEAD]
            l_ref[h:h + 1, cols] = alpha * l_ref[h:h + 1, cols] + pv[V_HEAD:V_HEAD + 1]
        for h in range(MLA_HEADS):
            mmin_ref[h] = jnp.min(m_ref[h:h + 1, :])

    if tk == sw and q_off % sw == 0:
        b = pl.program_id(0)
        fast = ki > 0
        for h in range(MLA_HEADS):
            q2 = qn2_ref[((b * nt_a + qi * nq_a) * MLA_HEADS) + h]
            for j in range(1, nq_a):
                q2 = jnp.maximum(q2, qn2_ref[((b * nt_a + qi * nq_a + j) * MLA_HEADS) + h])
            k2 = kn2_ref[((b * nt_a + ki * nk_a) * MLA_HEADS) + h]
            for j in range(1, nk_a):
                k2 = jnp.maximum(k2, kn2_ref[((b * nt_a + ki * nk_a + j) * MLA_HEADS) + h])
            room = mmin_ref[h] + EXP_RANGE
            fast = fast & (room > 0.0) & (q2 * k2 * NORM_MARGIN <= room * room)
        slow = jnp.logical_not(fast)
        rel = ki - (q_off // sw + qi * n_strips)
        pl.when(fast & (rel < 0))(lambda: fast_step(["full"] * n_strips))
        for r in range(n_strips):
            modes = ["skip"] * r + ["masked"] + ["full"] * (n_strips - r - 1)
            pl.when(fast & (rel == r))(functools.partial(fast_step, modes))
        pl.when(slow)(lambda: step(["masked"] * n_strips))
    else:
        step(["masked"] * n_strips)

    @pl.when(last_ref[p_id] == 1)
    def _():
        inv = 1.0 / l_ref[...]
        parts = [acc_ref[h * V_HEAD:(h + 1) * V_HEAD, :] * inv[h:h + 1, :] for h in range(MLA_HEADS)]
        o_ref[0] = jnp.concatenate(parts, axis=0).T.astype(BF16)


def _attn(qt, k, vt, qn2, kn2, tq, tk, sw, q_off):
    B, nqf, Tq = qt.shape
    Tk = k.shape[1]
    nq, nk = Tq // tq, Tk // tk
    nt_a = qn2.shape[1]
    norm_tiles = (max(1, nt_a // nq), max(1, nt_a // nk), nt_a)
    qi_l, ki_l, last_l = [], [], []
    for qi in range(nq):
        last_chunk = (q_off + qi * tq + tq - 1) // CHUNK
        kmax = min(nk - 1, ((last_chunk + 1) * CHUNK - 1) // tk)
        for ki in range(kmax + 1):
            qi_l.append(qi), ki_l.append(ki), last_l.append(int(ki == kmax))
    tabs = [jnp.asarray(np.asarray(a, np.int32)) for a in (qi_l, ki_l, last_l)]
    tabs += [qn2.reshape(-1), kn2.reshape(-1)]
    grid_spec = pltpu.PrefetchScalarGridSpec(
        num_scalar_prefetch=5, grid=(B, len(qi_l)),
        in_specs=[pl.BlockSpec((1, nqf, tq), lambda b, p, qi, ki, *_: (b, 0, qi[p])),
                  pl.BlockSpec((1, tk, MLA_HEADS * HEAD_PAD), lambda b, p, qi, ki, *_: (b, ki[p], 0)),
                  pl.BlockSpec((1, MLA_HEADS * V_HEAD, tk), lambda b, p, qi, ki, *_: (b, 0, ki[p]))],
        out_specs=pl.BlockSpec((1, tq, MLA_HEADS * V_HEAD), lambda b, p, qi, ki, *_: (b, qi[p], 0)),
        scratch_shapes=[pltpu.VMEM((MLA_HEADS, tq), F32), pltpu.VMEM((MLA_HEADS, tq), F32),
                        pltpu.VMEM((MLA_HEADS * V_HEAD, tq), F32), pltpu.VMEM((2, tk, sw), F32),
                        pltpu.VMEM((2, tk, sw), BF16), pltpu.SMEM((MLA_HEADS,), F32)])
    return pl.pallas_call(
        functools.partial(_attn_kernel, tq, tk, sw, q_off, norm_tiles), grid_spec=grid_spec,
        out_shape=jax.ShapeDtypeStruct((B, Tq, MLA_HEADS * V_HEAD), BF16),
        compiler_params=pltpu.CompilerParams(dimension_semantics=("parallel", "arbitrary"),
                                             vmem_limit_bytes=VMEM_LIMIT),
        name="attn",
    )(*tabs, qt, k, vt)


def _gla_kernel(lc, n_chunks, nch, gq_ref, gk_ref, lg_ref, gv_ref, gg_ref, gon_ref, s0_ref, og_ref, sfin_ref,
                st_ref):
    n_sub = lc // SUB
    n_stack = SUB * n_sub * (n_sub - 1) // 2
    stack_pad = -(-n_stack // LANES) * LANES

    @pl.when(pl.program_id(1) == 0)
    def _():
        st_ref[...] = s0_ref[0]

    lane_k = lax.broadcasted_iota(jnp.int32, (lc, GLA_K), 1) // GLA_DK
    ri = lax.broadcasted_iota(jnp.int32, (lc, lc), 0)
    ci = lax.broadcasted_iota(jnp.int32, (lc, lc), 1)
    tri = (ci <= ri).astype(BF16)
    e_r = lax.broadcasted_iota(jnp.int32, (GLA_K, GLA_V), 0) // GLA_DK
    e_c = lax.broadcasted_iota(jnp.int32, (GLA_K, GLA_V), 1) // GLA_DV
    e_mat = (e_r == e_c).astype(BF16)
    sub_row = lax.broadcasted_iota(jnp.int32, (SUB, GLA_K), 0)
    st_lane = lax.broadcasted_iota(jnp.int32, (GLA_DV, GLA_K), 1) // GLA_DK
    a_row = lax.broadcasted_iota(jnp.int32, (lc, stack_pad), 0) // SUB
    a_col = lax.broadcasted_iota(jnp.int32, (lc, stack_pad), 1)
    off_mask = jnp.zeros((lc, stack_pad), jnp.bool_)
    seg = 0
    for i_sub in range(1, n_sub):
        off_mask = off_mask | ((a_row == i_sub) & (a_col >= seg) & (a_col < seg + SUB * i_sub))
        seg += SUB * i_sub

    def heads_on_rows(a):
        return jnp.concatenate([jnp.where(lane_k == h, a, 0.0) for h in range(GLA_HEADS)], axis=0).astype(BF16)

    def load(c):
        r = pl.ds(pl.multiple_of(c * lc, lc), lc)
        lg = lg_ref[0, r, :]
        hi = lg.astype(BF16)
        r1 = lg - hi.astype(F32)
        mid = r1.astype(BF16)
        lo = (r1 - mid.astype(F32)).astype(BF16)
        b = _dot(tri, hi) + _dot(tri, mid) + _dot(tri, lo)
        return dict(r=r, q=gq_ref[0, r, :], k=gk_ref[0, r, :], v=gv_ref[0, r, :].astype(F32), b=b)

    def local(d):
        q, k, v, b = d["q"], d["k"], d["v"], d["b"]
        v_b = v.astype(BF16)
        b_last = b[lc - 1:lc, :]
        d["decay"] = jnp.exp(b_last)
        d["q4"] = heads_on_rows(q * jnp.exp(b))
        kd = (k * jnp.exp(b_last - b)).astype(BF16)
        full = _tn(v_b, kd)
        upd = jnp.zeros((GLA_DV, GLA_K), F32)
        for h in range(GLA_HEADS):
            upd = upd + jnp.where(st_lane == h, full[h * GLA_DV:(h + 1) * GLA_DV], 0.0)
        d["upd"] = upd
        refs = [jnp.zeros((SUB, GLA_K), F32)]
        kst, vst = [], []
        for i_sub in range(1, n_sub):
            r_i = b[SUB * i_sub - 1:SUB * i_sub, :]
            refs.append(jnp.broadcast_to(r_i, (SUB, GLA_K)))
            kst.append(k[:SUB * i_sub] * jnp.exp(r_i - b[:SUB * i_sub]))
            vst.append(v[:SUB * i_sub])
        if stack_pad > n_stack:
            kst.append(jnp.zeros((stack_pad - n_stack, GLA_K), F32))
            vst.append(jnp.zeros((stack_pad - n_stack, GLA_V), F32))
        kst = jnp.concatenate(kst, axis=0).astype(BF16)
        vst = jnp.concatenate(vst, axis=0).astype(BF16)
        qp = q * jnp.exp(b - jnp.concatenate(refs, axis=0))
        a4 = _nt(heads_on_rows(qp), kst)
        d["o_off"] = [
            _dot(jnp.where(off_mask, a4[h * lc:(h + 1) * lc], 0.0).astype(BF16), vst[:, h * GLA_DV:(h + 1) * GLA_DV])
            for h in range(GLA_HEADS)]

    def same_sub(d):
        q, k, v, b = d["q"], d["k"], d["v"], d["b"]
        terms = []
        for i_sub in range(n_sub):
            rs = slice(SUB * i_sub, SUB * (i_sub + 1))
            q_i, k_i, b_i = q[rs], k[rs], b[rs]
            for j in range(SUB):
                arg = jnp.where(sub_row >= j, b_i - b_i[j:j + 1], NEG)
                terms.append(q_i * (k_i[j:j + 1] * jnp.exp(arg)))
        w = _dot(jnp.concatenate(terms, axis=0).astype(BF16), e_mat)
        o_diag = []
        for i_sub in range(n_sub):
            acc = None
            for j in range(SUB):
                row = SUB * i_sub + j
                t = w[row * SUB:(row + 1) * SUB] * v[row:row + 1]
                acc = t if acc is None else acc + t
            o_diag.append(acc)
        d["o_diag"] = jnp.concatenate(o_diag, axis=0)

    def carried(d):
        st = st_ref[...]
        d["o_heads"] = _nt(d["q4"], st.astype(BF16))
        st_ref[...] = d["decay"] * st + d["upd"]

    def finish(d):
        o = jnp.concatenate(
            [d["o_heads"][h * lc:(h + 1) * lc] + d["o_off"][h] for h in range(GLA_HEADS)], axis=1) + d["o_diag"]
        gon = gon_ref[...]
        normed = jnp.concatenate(
            [_rms(o[:, h * GLA_DV:(h + 1) * GLA_DV], gon) for h in range(GLA_HEADS)], axis=1)
        gg = gg_ref[0, d["r"], :]
        og_ref[0, d["r"], :] = (normed * (gg * jax.nn.sigmoid(gg))).astype(BF16)

    def body(it, carry):
        ds = [load(it * nch + u) for u in range(nch)]
        for stage in (local, same_sub, carried, finish):
            for d in ds:
                stage(d)
        return carry

    lax.fori_loop(0, n_chunks // nch, body, 0)

    @pl.when(pl.program_id(1) == pl.num_programs(1) - 1)
    def _():
        sfin_ref[0] = st_ref[...]


def _gla(gq, gk, lg, gv, gg, g_on, s0t, lc, tg):
    B, T, _ = gq.shape
    ng = T // tg
    tok = lambda w: pl.BlockSpec((1, tg, w), lambda b, g: (b, g, 0))
    st_spec = pl.BlockSpec((1, GLA_DV, GLA_K), lambda b, g: (b, 0, 0))
    S = jax.ShapeDtypeStruct
    return pl.pallas_call(
        functools.partial(_gla_kernel, lc, tg // lc, _pick(tg // lc, (4, 2))), grid=(B, ng),
        in_specs=[tok(GLA_K), tok(GLA_K), tok(GLA_K), tok(GLA_V), tok(GLA_V),
                  pl.BlockSpec((1, GLA_DV), lambda b, g: (0, 0)), st_spec],
        out_specs=[tok(GLA_V), st_spec],
        out_shape=[S((B, T, GLA_V), BF16), S((B, GLA_DV, GLA_K), F32)],
        scratch_shapes=[pltpu.VMEM((GLA_DV, GLA_K), F32)],
        compiler_params=pltpu.CompilerParams(dimension_semantics=("parallel", "arbitrary"),
                                             vmem_limit_bytes=VMEM_LIMIT),
        name="gla",
    )(gq, gk, lg, gv, gg, g_on, s0t)


def _ffn_kernel(final, x_ref, om_ref, og_ref, wo_ref, gf_ref, wg_ref, wu_ref, wd_ref, gfin_ref, y_ref):
    cat = jnp.concatenate([om_ref[0], og_ref[0]], axis=1)
    x1 = x_ref[0] + _dot(cat, wo_ref[...])
    hn = _rms(x1, gf_ref[...]).astype(BF16)
    a = _dot(hn, wg_ref[...])
    u = _dot(hn, wu_ref[...])
    act = (a * jax.nn.sigmoid(a) * u).astype(BF16)
    x2 = x1 + _dot(act, wd_ref[...])
    y_ref[0] = _rms(x2, gfin_ref[...]) if final else x2


def _ffn(x, o_mla, o_g, lw, l, w_gu, w_down, g_final, tm, final):
    B, T, D = x.shape
    tok = lambda w: pl.BlockSpec((1, tm, w), lambda b, t: (b, t, 0))
    const = lambda a: pl.BlockSpec(a.shape, lambda b, t: (0,) * a.ndim, pipeline_mode=pl.Buffered(1))
    d_ff = w_down.shape[1]
    half = lambda j: pl.BlockSpec((None, D, d_ff), lambda b, t: (l, 0, j), pipeline_mode=pl.Buffered(1))
    down = pl.BlockSpec((None, d_ff, D), lambda b, t: (l, 0, 0), pipeline_mode=pl.Buffered(1))
    weights = (lw["w_o"], lw["g_ffn"], w_gu, w_gu, w_down, g_final)
    w_specs = [const(lw["w_o"]), const(lw["g_ffn"]), half(0), half(1), down, const(g_final)]
    return pl.pallas_call(
        functools.partial(_ffn_kernel, final), grid=(B, T // tm),
        in_specs=[tok(D), tok(o_mla.shape[-1]), tok(o_g.shape[-1])] + w_specs,
        out_specs=tok(D), out_shape=jax.ShapeDtypeStruct((B, T, D), F32),
        compiler_params=pltpu.CompilerParams(dimension_semantics=("parallel", "parallel"),
                                             vmem_limit_bytes=VMEM_LIMIT),
        name="ffn_final" if final else "ffn",
    )(x, o_mla, o_g, *weights)


def _layer_weights(l, g_attn, w_in, g_qn, w_uq, g_kvn, w_ukv, w_a2, b_a2, g_gla_on, w_o, g_ffn):
    d_model = w_in.shape[1]
    wi = w_in[l]
    c = np.cumsum([0, Q_LORA, KV_LORA, QK_ROPE, GLA_K, GLA_K, GLA_V, GLA_V, GATE_LR])
    sm = jnp.concatenate([wi[:, c[2]:c[3]], wi[:, c[7]:c[8]],
                          jnp.zeros((d_model, LANES - QK_ROPE - GATE_LR), F32)], axis=1)
    w_in_r = jnp.concatenate([wi[:, c[0]:c[2]], wi[:, c[3]:c[4]] * (GLA_DK ** -0.5), wi[:, c[4]:c[7]], sm], axis=1)
    wq = w_uq[l].reshape(Q_LORA, MLA_HEADS, QK_NOPE + QK_ROPE)
    w_qn_t = wq[:, :, :QK_NOPE].reshape(Q_LORA, -1).T
    w_qr_t = jnp.concatenate([wq[:, :, QK_NOPE:QK_NOPE + ROPE_HALF].reshape(Q_LORA, -1),
                              wq[:, :, QK_NOPE + ROPE_HALF:].reshape(Q_LORA, -1)], axis=1).T
    wkv = w_ukv[l].reshape(KV_LORA, MLA_HEADS, QK_NOPE + V_HEAD)
    wk_nope = jnp.pad(wkv[:, :, :QK_NOPE], ((0, 0), (0, 0), (0, HEAD_PAD - QK_NOPE))).reshape(KV_LORA, -1)
    place = np.zeros((LANES, MLA_HEADS, HEAD_PAD), np.float32)
    for d in range(QK_ROPE):
        place[d, :, QK_NOPE + d] = 1.0
    w_k = jnp.concatenate([wk_nope, jnp.asarray(place.reshape(LANES, -1))], axis=0)
    w_v_t = wkv[:, :, QK_NOPE:].reshape(KV_LORA, -1).T
    w_uk_t = wkv[:, :, :QK_NOPE].transpose(1, 2, 0)
    w_uv_pad = jnp.zeros((MLA_HEADS, KV_LORA, MLA_HEADS * V_HEAD), F32)
    for hd in range(MLA_HEADS):
        w_uv_pad = w_uv_pad.at[hd, :, hd * V_HEAD:(hd + 1) * V_HEAD].set(wkv[:, hd, QK_NOPE:])
    w_a2_p = jnp.zeros((LANES, GLA_K), F32).at[QK_ROPE:QK_ROPE + GATE_LR].set(w_a2[l])
    row = lambda v: v.reshape(1, -1)
    n_nope, n_rope = MLA_HEADS * QK_NOPE, MLA_HEADS * ROPE_HALF
    e_q = np.zeros((BF16_ROWS, n_nope + 2 * n_rope), np.float32)
    e_k = np.zeros((MLA_HEADS * HEAD_PAD, LANES), np.float32)
    for h in range(MLA_HEADS):
        e_q[h, h * QK_NOPE:(h + 1) * QK_NOPE] = 1.0
        e_q[h, n_nope + h * ROPE_HALF:n_nope + (h + 1) * ROPE_HALF] = 1.0
        e_q[h, n_nope + n_rope + h * ROPE_HALF:n_nope + n_rope + (h + 1) * ROPE_HALF] = 1.0
        e_k[h * HEAD_PAD:(h + 1) * HEAD_PAD, h] = 1.0
    return {
        "e_q": jnp.asarray(e_q, BF16), "e_k": jnp.asarray(e_k, BF16),
        "g_attn": row(g_attn[l]), "w_in": w_in_r.astype(BF16),
        "g_qn": row(g_qn[l] * (ATTN_SCALE * LOG2E)), "w_qnT": w_qn_t.astype(BF16), "w_qrT": w_qr_t.astype(BF16),
        "g_kvn": row(g_kvn[l]), "w_k": w_k.astype(BF16), "w_vT": w_v_t.astype(BF16),
        "w_ukT": w_uk_t.astype(BF16), "w_uv_pad": w_uv_pad.astype(BF16),
        "w_a2": w_a2_p.astype(BF16), "b_a2": row(b_a2[l]),
        "g_on": row(g_gla_on[l]), "w_o": w_o[l].astype(BF16), "g_ffn": row(g_ffn[l]),
    }


def _rope_tables(pos_off, pos_base):
    inv = ROPE_THETA ** (-jnp.arange(ROPE_HALF, dtype=F32) / ROPE_HALF)

    def tables(pos):
        ang = pos.astype(F32)[:, None] * inv[None, :]
        cos, sin = jnp.cos(ang), jnp.sin(ang)
        zeros = jnp.zeros((pos.shape[0], LANES - QK_ROPE), F32)
        rows = [jnp.concatenate([a, a, zeros], axis=1) for a in (cos, sin)]
        cols = [jnp.tile(a, (1, MLA_HEADS)).T for a in (cos, sin)]
        return rows, cols

    (cro, sro), (cto, sto) = tables(pos_off)
    (crb, srb), (ctb, stb) = tables(pos_base)
    nt = pos_base.shape[0]
    base_r = lambda a: a.reshape(nt, 1, LANES)
    base_t = lambda a: jnp.broadcast_to(a.T[:, :, None], (nt, LANES, LANES))
    return cro, sro, base_r(crb), base_r(srb), cto, sto, base_t(ctb), base_t(stb)


def _pick(n, pref):
    for t in pref:
        if n % t == 0:
            return t
    return n


def kernel(x_prompt, x_sample, cache_ckv, cache_krope, state_gla, g_attn, w_in, g_qn, w_uq, g_kvn, w_ukv, w_a2, b_a2,
           g_gla_on, w_o, g_ffn, w_gu, w_down, g_final):
    depth = w_in.shape[0]
    B, T, D = x_prompt.shape
    nb, ts, _ = x_sample.shape
    past = cache_ckv.shape[2]
    assert T % CHUNK == 0 and ts % BF16_ROWS == 0 and ts <= CHUNK

    tm_p = _pick(T, (512, 256, 128))
    tabs_p = _rope_tables(jnp.arange(tm_p, dtype=jnp.int32), jnp.arange(T // tm_p, dtype=jnp.int32) * tm_p)
    tabs_s = _rope_tables(jnp.tile(past + jnp.arange(ts, dtype=jnp.int32), nb), jnp.zeros((1,), jnp.int32))
    w_gu_b, w_down_b = w_gu.astype(BF16), w_down.astype(BF16)
    tk = _pick(T, (ATTN_TK, 256, 128, 64))
    tq = ATTN_TQ if (T % ATTN_TQ == 0 and ATTN_TQ % tk == 0) else tk
    tg = _pick(T, (512, 256, 128, 64))
    g_fin = g_final.reshape(1, -1)

    xp = x_prompt
    xs = x_sample.reshape(1, nb * ts, D)
    outs = {k: [] for k in ("ckv_p", "kr_p", "st_p", "ckv_s", "kr_s", "st_s")}
    for l in range(depth):
        lw = _layer_weights(l, g_attn, w_in, g_qn, w_uq, g_kvn, w_ukv, w_a2, b_a2, g_gla_on, w_o, g_ffn)
        final = l == depth - 1
        ckv, kr, qt, k, vt, qn2, kn2, gq, gk, gv, gg, lg = _proj(xp, tabs_p, lw, tm_p, True)
        o_mla = _attn(qt, k, vt, qn2[:, :, :MLA_HEADS, 0], kn2[:, :, 0, :MLA_HEADS], tq, tk, tk, 0)
        s0 = jnp.zeros((B, GLA_DV, GLA_K), F32)
        o_g, st_fin = _gla(gq, gk, lg, gv, gg, lw["g_on"], s0, CHUNK, tg)
        xp = _ffn(xp, o_mla, o_g, lw, l, w_gu_b, w_down_b, g_fin, tm_p, final)
        outs["ckv_p"].append(ckv), outs["kr_p"].append(kr)
        outs["st_p"].append(st_fin.reshape(B, GLA_DV, GLA_HEADS, GLA_DK).transpose(0, 2, 3, 1))
        ckv, kr, qt, gq, gk, gv, gg, lg = _proj(xs, tabs_s, lw, nb * ts, False)
        ckv, kr = ckv.reshape(nb, ts, -1), kr.reshape(nb, ts, -1)
        n_nope, n_rope = MLA_HEADS * QK_NOPE, MLA_HEADS * ROPE_HALF
        per_head = lambda a: a.reshape(MLA_HEADS, -1, nb * ts)
        qt = jnp.concatenate([per_head(qt[0, :n_nope]), per_head(qt[0, n_nope:n_nope + n_rope]),
                              per_head(qt[0, n_nope + n_rope:]),
                              jnp.zeros((MLA_HEADS, HEAD_PAD - QK_NOPE - QK_ROPE, nb * ts), BF16)], axis=1)
        qt = qt.reshape(MLA_HEADS, HEAD_PAD, nb, ts).transpose(2, 0, 1, 3)
        o_mla = _cache_attn(l, cache_ckv, cache_krope, ckv, kr, qt, lw).reshape(1, nb * ts, -1)
        s0 = state_gla[l].astype(F32).transpose(0, 3, 1, 2).reshape(nb, GLA_DV, GLA_K)
        per_b = lambda a: a.reshape(nb, ts, -1)
        o_g, st_fin = _gla(per_b(gq), per_b(gk), per_b(lg), per_b(gv), per_b(gg), lw["g_on"], s0, ts, ts)
        xs = _ffn(xs, o_mla, o_g.reshape(1, nb * ts, -1), lw, l, w_gu_b, w_down_b, g_fin, nb * ts, final)
        outs["ckv_s"].append(ckv), outs["kr_s"].append(kr)
        outs["st_s"].append(st_fin.reshape(nb, GLA_DV, GLA_HEADS, GLA_DK).transpose(0, 2, 3, 1))

    stack = lambda key: jnp.stack(outs[key])
    return (xp, xs.reshape(nb, ts, D), stack("ckv_p"), stack("kr_p"), stack("st_p"),
            stack("ckv_s"), stack("kr_s"), stack("st_s"))
```

```python
import functools

import jax
import jax.numpy as jnp
import numpy as np
from jax import lax
from jax.experimental import pallas as pl
from jax.experimental.pallas import tpu as pltpu

F32 = jnp.float32
BF16 = jnp.bfloat16

CHUNK = 64
EPS = 1e-6
MLA_HEADS = 8
QK_NOPE = 64
QK_ROPE = 32
ROPE_HALF = QK_ROPE // 2
V_HEAD = 64
Q_LORA = 256
KV_LORA = 256
ROPE_THETA = 10000.0
ATTN_SCALE = (QK_NOPE + QK_ROPE) ** -0.5
LOG2E = 1.4426950408889634
GLA_HEADS = 4
GLA_DK = 64
GLA_DV = 128
GATE_LR = 16
GATE_TAU = 16.0
GLA_K = GLA_HEADS * GLA_DK
GLA_V = GLA_HEADS * GLA_DV

LANES = 128
BF16_ROWS = 16
HEAD_PAD = 128
PV_ROWS = 256
ATTN_TK = 512
ATTN_TQ = 1024
SUB = 8
NEG = -1e30
EXP_RANGE = 64.0
NORM_MARGIN = 1.05
VMEM_LIMIT = 48 * 1024 * 1024

Z_CQ = 0
Z_CKV = Z_CQ + Q_LORA
Z_GQ = Z_CKV + KV_LORA
Z_GK = Z_GQ + GLA_K
Z_GV = Z_GK + GLA_K
Z_GG = Z_GV + GLA_V
Z_SM = Z_GG + GLA_V
Z_W = Z_SM + LANES


def _nt(a, b):
    return lax.dot_general(a, b, (((1,), (1,)), ((), ())), preferred_element_type=F32)


def _tn(a, b):
    return lax.dot_general(a, b, (((0,), (0,)), ((), ())), preferred_element_type=F32)


def _dot(a, b):
    return jnp.dot(a, b, preferred_element_type=F32)


def _rms(x, g):
    return x * lax.rsqrt(jnp.mean(x * x, axis=-1, keepdims=True) + EPS) * g


def _proj_kernel(emit_kv, x_ref, cro_ref, sro_ref, crb_ref, srb_ref, cto_ref, sto_ref, ctb_ref, stb_ref, gat_ref,
                 win_ref, gqn_ref, wqn_ref, wqr_ref, gkv_ref, wk_ref, wvt_ref, wa2_ref, ba2_ref, eq_ref, ek_ref, *outs):
    if emit_kv:
        ckv_ref, kr_ref, qt_ref, k_ref, vt_ref, qn2_ref, kn2_ref, gq_ref, gk_ref, gv_ref, gg_ref, lg_ref = outs
    else:
        ckv_ref, kr_ref, qt_ref, gq_ref, gk_ref, gv_ref, gg_ref, lg_ref = outs
    x = x_ref[0]
    h = _rms(x, gat_ref[...]).astype(BF16)
    z = _dot(h, win_ref[...])

    cqn = _rms(z[:, Z_CQ:Z_CQ + Q_LORA], gqn_ref[...]).astype(BF16)
    n_nope = MLA_HEADS * QK_NOPE
    n_rope = MLA_HEADS * ROPE_HALF
    q_nope = _nt(wqn_ref[...], cqn)
    qt_ref[0, 0:n_nope, :] = q_nope.astype(BF16)
    qr = _nt(wqr_ref[...], cqn)
    x1, x2 = qr[:n_rope], qr[n_rope:]
    tm = qr.shape[1]
    cb, sb = jnp.broadcast_to(ctb_ref[0, :, 0:1], (n_rope, tm)), jnp.broadcast_to(stb_ref[0, :, 0:1], (n_rope, tm))
    ct = cto_ref[...] * cb - sto_ref[...] * sb
    st = sto_ref[...] * cb + cto_ref[...] * sb
    q_r1, q_r2 = x1 * ct - x2 * st, x1 * st + x2 * ct
    qt_ref[0, n_nope:n_nope + n_rope, :] = q_r1.astype(BF16)
    qt_ref[0, n_nope + n_rope:, :] = q_r2.astype(BF16)

    ckv = _rms(z[:, Z_CKV:Z_CKV + KV_LORA], gkv_ref[...])
    ckv_ref[0] = ckv
    u = z[:, Z_SM:Z_SM + LANES]
    lane = lax.broadcasted_iota(jnp.int32, u.shape, 1)
    partner = jnp.where(lane < ROPE_HALF, pltpu.roll(u, LANES - ROPE_HALF, 1), pltpu.roll(u, ROPE_HALF, 1))
    cos_r = cro_ref[...] * crb_ref[0] - sro_ref[...] * srb_ref[0]
    sin_r = sro_ref[...] * crb_ref[0] + cro_ref[...] * srb_ref[0]
    kr = u * cos_r + partner * jnp.where(lane < ROPE_HALF, -sin_r, sin_r)
    kr_ref[0] = kr[:, :QK_ROPE]
    if emit_kv:
        ckv_b = ckv.astype(BF16)
        k_all = _dot(ckv_b, wk_ref[...]) + jnp.concatenate([pltpu.roll(kr, QK_NOPE, 1)] * MLA_HEADS, axis=1)
        k_ref[0] = k_all.astype(BF16)
        vt_ref[0] = _nt(wvt_ref[...], ckv_b).astype(BF16)
        q_sq = jnp.concatenate([q_nope * q_nope, q_r1 * q_r1, q_r2 * q_r2], axis=0).astype(BF16)
        qn2 = jnp.max(_dot(eq_ref[...], q_sq), axis=1, keepdims=True)
        qn2_ref[0, 0] = jnp.broadcast_to(qn2, qn2_ref.shape[2:])
        kn2 = _dot((k_all * k_all).astype(BF16), ek_ref[...])
        kn2_ref[0, 0] = jnp.max(kn2, axis=0, keepdims=True)

    gq_ref[0] = z[:, Z_GQ:Z_GQ + GLA_K]
    gk_ref[0] = z[:, Z_GK:Z_GK + GLA_K]
    gv_ref[0] = z[:, Z_GV:Z_GV + GLA_V].astype(BF16)
    gg_ref[0] = z[:, Z_GG:Z_GG + GLA_V]
    xg = _dot(u.astype(BF16), wa2_ref[...]) + ba2_ref[...]
    lg_ref[0] = (jnp.minimum(xg, 0.0) - jnp.log1p(jnp.exp(-jnp.abs(xg)))) * (1.0 / GATE_TAU)


def _proj(x, tabs, lw, tm, emit_kv):
    B, T, D = x.shape
    nt = T // tm
    tok = lambda w: pl.BlockSpec((1, tm, w), lambda b, t: (b, t, 0))
    feat = lambda r: pl.BlockSpec((1, r, tm), lambda b, t: (b, 0, t))
    full = lambda a: pl.BlockSpec(a.shape, lambda b, t: (0,) * a.ndim)
    weights = (lw["g_attn"], lw["w_in"], lw["g_qn"], lw["w_qnT"], lw["w_qrT"], lw["g_kvn"], lw["w_k"], lw["w_vT"],
               lw["w_a2"], lw["b_a2"], lw["e_q"], lw["e_k"])
    off_r = pl.BlockSpec((tm, LANES), lambda b, t: (0, 0))
    base_r = pl.BlockSpec((1, 1, LANES), lambda b, t: (t, 0, 0))
    off_t = pl.BlockSpec((LANES, tm), lambda b, t: (0, 0))
    base_t = pl.BlockSpec((1, LANES, LANES), lambda b, t: (t, 0, 0))
    in_specs = [tok(D), off_r, off_r, base_r, base_r, off_t, off_t, base_t, base_t]
    in_specs += [full(w) for w in weights]
    S = jax.ShapeDtypeStruct
    nq = MLA_HEADS * (QK_NOPE + QK_ROPE)
    out_shape = [S((B, T, KV_LORA), F32), S((B, T, QK_ROPE), F32), S((B, nq, T), BF16)]
    out_specs = [tok(KV_LORA), tok(QK_ROPE), feat(nq)]
    if emit_kv:
        out_shape += [S((B, T, MLA_HEADS * HEAD_PAD), BF16), S((B, MLA_HEADS * V_HEAD, T), BF16),
                      S((B, nt, BF16_ROWS, LANES), F32), S((B, nt, 1, LANES), F32)]
        out_specs += [tok(MLA_HEADS * HEAD_PAD), feat(MLA_HEADS * V_HEAD),
                      pl.BlockSpec((1, 1, BF16_ROWS, LANES), lambda b, t: (b, t, 0, 0)),
                      pl.BlockSpec((1, 1, 1, LANES), lambda b, t: (b, t, 0, 0))]
    out_shape += [S((B, T, GLA_K), F32), S((B, T, GLA_K), F32), S((B, T, GLA_V), BF16), S((B, T, GLA_V), F32),
                  S((B, T, GLA_K), F32)]
    out_specs += [tok(GLA_K), tok(GLA_K), tok(GLA_V), tok(GLA_V), tok(GLA_K)]
    return pl.pallas_call(
        functools.partial(_proj_kernel, emit_kv),
        grid=(B, nt), in_specs=in_specs, out_specs=out_specs, out_shape=out_shape,
        compiler_params=pltpu.CompilerParams(dimension_semantics=("parallel", "parallel"),
                                             vmem_limit_bytes=VMEM_LIMIT),
        name="proj_kv" if emit_kv else "proj",
    )(x, *tabs, *weights)


def _cache_attn_kernel(past, cckv_ref, ckr_ref, nckv_ref, nkr_ref, qt_ref, wuk_ref, wuv_ref, o_ref):
    ts = qt_ref.shape[-1]
    ckv = jnp.concatenate([cckv_ref[0], nckv_ref[0]], axis=0).astype(BF16)
    kr = jnp.concatenate([ckr_ref[0], nkr_ref[0]], axis=0).astype(BF16)
    eye = (lax.broadcasted_iota(jnp.int32, (QK_ROPE, QK_ROPE), 0)
           == lax.broadcasted_iota(jnp.int32, (QK_ROPE, QK_ROPE), 1)).astype(BF16)
    q_lat, q_rope = [], []
    for h in range(MLA_HEADS):
        q_lat.append(_tn(qt_ref[0, h, 0:QK_NOPE, :], wuk_ref[h]).astype(BF16))
        q_rope.append(_tn(qt_ref[0, h, QK_NOPE:QK_NOPE + QK_ROPE, :], eye).astype(BF16))
    s = _nt(jnp.concatenate(q_lat, axis=0), ckv) + _nt(jnp.concatenate(q_rope, axis=0), kr)
    kpos = lax.broadcasted_iota(jnp.int32, s.shape, 1)
    qpos = past + lax.broadcasted_iota(jnp.int32, s.shape, 0) % ts
    s = jnp.where((kpos // CHUNK) <= (qpos // CHUNK), s, NEG)
    p = jnp.exp2(s - jnp.max(s, axis=1, keepdims=True))
    o_lat = _dot(p.astype(BF16), ckv) / jnp.sum(p, axis=1, keepdims=True)
    o = _dot(o_lat[0:ts].astype(BF16), wuv_ref[0])
    for h in range(1, MLA_HEADS):
        o = o + _dot(o_lat[h * ts:(h + 1) * ts].astype(BF16), wuv_ref[h])
    o_ref[0] = o.astype(BF16)


def _cache_attn(l, cache_ckv, cache_kr, new_ckv, new_kr, qt, lw):
    _, nb, past, _ = cache_ckv.shape
    ts = new_ckv.shape[1]
    per_b = lambda a: pl.BlockSpec((1,) + a.shape[1:], lambda b: (b,) + (0,) * (a.ndim - 1))
    layer_b = lambda a: pl.BlockSpec((None, 1) + a.shape[2:], lambda b: (l, b, 0, 0))
    full = lambda a: pl.BlockSpec(a.shape, lambda b: (0,) * a.ndim)
    return pl.pallas_call(
        functools.partial(_cache_attn_kernel, past), grid=(nb,),
        in_specs=[layer_b(cache_ckv), layer_b(cache_kr), per_b(new_ckv), per_b(new_kr), per_b(qt), full(lw["w_ukT"]),
                  full(lw["w_uv_pad"])],
        out_specs=pl.BlockSpec((1, ts, MLA_HEADS * V_HEAD), lambda b: (b, 0, 0)),
        out_shape=jax.ShapeDtypeStruct((nb, ts, MLA_HEADS * V_HEAD), BF16),
        compiler_params=pltpu.CompilerParams(dimension_semantics=("parallel",), vmem_limit_bytes=VMEM_LIMIT),
        name="cache_attn",
    )(cache_ckv, cache_kr, new_ckv, new_kr, qt, lw["w_ukT"], lw["w_uv_pad"])


def _attn_kernel(tq, tk, sw, q_off, norm_tiles, qi_ref, ki_ref, last_ref, qn2_ref, kn2_ref, qt_ref, k_ref, vt_ref,
                 o_ref, m_ref, l_ref, acc_ref, s_ref, p_ref, mmin_ref):
    nq_a, nk_a, nt_a = norm_tiles
    p_id = pl.program_id(1)
    qi, ki = qi_ref[p_id], ki_ref[p_id]
    n_nope = MLA_HEADS * QK_NOPE
    n_rope = MLA_HEADS * ROPE_HALF

    @pl.when(ki == 0)
    def _():
        m_ref[...] = jnp.full(m_ref.shape, NEG, F32)
        l_ref[...] = jnp.zeros(l_ref.shape, F32)
        acc_ref[...] = jnp.zeros(acc_ref.shape, F32)

    rb = PV_ROWS if tk % PV_ROWS == 0 else tk
    ones_rows = jnp.ones((BF16_ROWS, rb), BF16)

    n_strips = tq // sw

    def qk(h, i, masked):
        cols = slice(i * sw, (i + 1) * sw)
        qt_h = jnp.concatenate(
            [qt_ref[0, h * QK_NOPE:(h + 1) * QK_NOPE, cols],
             qt_ref[0, n_nope + h * ROPE_HALF:n_nope + (h + 1) * ROPE_HALF, cols],
             qt_ref[0, n_nope + n_rope + h * ROPE_HALF:n_nope + n_rope + (h + 1) * ROPE_HALF, cols],
             jnp.zeros((HEAD_PAD - QK_NOPE - QK_ROPE, sw), BF16)], axis=0)
        s = _dot(k_ref[0, :, h * HEAD_PAD:(h + 1) * HEAD_PAD], qt_h)
        if masked:
            kpos = ki * tk + lax.broadcasted_iota(jnp.int32, (tk, sw), 0)
            qpos = q_off + qi * tq + i * sw + lax.broadcasted_iota(jnp.int32, (tk, sw), 1)
            s = jnp.where((kpos // CHUNK) <= (qpos // CHUNK), s, NEG)
        return s

    def pv_rows(h, r):
        return jnp.concatenate([vt_ref[0, h * V_HEAD:(h + 1) * V_HEAD, r * rb:(r + 1) * rb], ones_rows], axis=0)

    def fast_step(modes):
        stages = [(h, i) for h in range(MLA_HEADS) for i in range(n_strips) if modes[i] != "skip"]

        def weights(t):
            h, i = stages[t]
            s = qk(h, i, modes[i] == "masked")
            p_ref[t % 2] = jnp.exp2(s - m_ref[h:h + 1, i * sw:(i + 1) * sw]).astype(BF16)

        weights(0)
        for t, (h, i) in enumerate(stages):
            if t + 1 < len(stages):
                weights(t + 1)
            pv = None
            for r in range(tk // rb):
                d = _dot(pv_rows(h, r), p_ref[t % 2, r * rb:(r + 1) * rb, :])
                pv = d if pv is None else pv + d
            rows, cols = slice(h * V_HEAD, (h + 1) * V_HEAD), slice(i * sw, (i + 1) * sw)
            acc_ref[rows, cols] = acc_ref[rows, cols] + pv[:V_HEAD]
            l_ref[h:h + 1, cols] = l_ref[h:h + 1, cols] + pv[V_HEAD:V_HEAD + 1]

    def step(modes):
        stages = [(h, i) for h in range(MLA_HEADS) for i in range(n_strips) if modes[i] != "skip"]

        def scores(t):
            h, i = stages[t]
            s = qk(h, i, modes[i] == "masked")
            s_ref[t % 2] = s
            return jnp.max(s, axis=0, keepdims=True)

        smax_next = scores(0)
        for t, (h, i) in enumerate(stages):
            smax = smax_next
            if t + 1 < len(stages):
                smax_next = scores(t + 1)
            cols = slice(i * sw, (i + 1) * sw)
            m_old = m_ref[h:h + 1, cols]
            m_new = jnp.maximum(m_old, smax)
            alpha = jnp.exp2(m_old - m_new)
            m_ref[h:h + 1, cols] = m_new
            rows = slice(h * V_HEAD, (h + 1) * V_HEAD)
            pv = None
            for r in range(tk // rb):
                p = jnp.exp2(s_ref[t % 2, r * rb:(r + 1) * rb, :] - m_new).astype(BF16)
                d = _dot(pv_rows(h, r), p)
                pv = d if pv is None else pv + d
            acc_ref[rows, cols] = alpha * acc_ref[rows, cols] + pv[:V_HEAD]
            l_ref[h:h + 1, cols] = alpha * l_ref[h:h + 1, cols] + pv[V_HEAD:V_HEAD + 1]
        for h in range(MLA_HEADS):
            mmin_ref[h] = jnp.min(m_ref[h:h + 1, :])

    if tk == sw and q_off % sw == 0:
        b = pl.program_id(0)
        fast = ki > 0
        for h in range(MLA_HEADS):
            q2 = qn2_ref[((b * nt_a + qi * nq_a) * MLA_HEADS) + h]
            for j in range(1, nq_a):
                q2 = jnp.maximum(q2, qn2_ref[((b * nt_a + qi * nq_a + j) * MLA_HEADS) + h])
            k2 = kn2_ref[((b * nt_a + ki * nk_a) * MLA_HEADS) + h]
            for j in range(1, nk_a):
                k2 = jnp.maximum(k2, kn2_ref[((b * nt_a + ki * nk_a + j) * MLA_HEADS) + h])
            room = mmin_ref[h] + EXP_RANGE
            fast = fast & (room > 0.0) & (q2 * k2 * NORM_MARGIN <= room * room)
        slow = jnp.logical_not(fast)
        rel = ki - (q_off // sw + qi * n_strips)
        pl.when(fast & (rel < 0))(lambda: fast_step(["full"] * n_strips))
        for r in range(n_strips):
            modes = ["skip"] * r + ["masked"] + ["full"] * (n_strips - r - 1)
            pl.when(fast & (rel == r))(functools.partial(fast_step, modes))
        pl.when(slow)(lambda: step(["masked"] * n_strips))
    else:
        step(["masked"] * n_strips)

    @pl.when(last_ref[p_id] == 1)
    def _():
        inv = 1.0 / l_ref[...]
        parts = [acc_ref[h * V_HEAD:(h + 1) * V_HEAD, :] * inv[h:h + 1, :] for h in range(MLA_HEADS)]
        o_ref[0] = jnp.concatenate(parts, axis=0).T.astype(BF16)


def _attn(qt, k, vt, qn2, kn2, tq, tk, sw, q_off):
    B, nqf, Tq = qt.shape
    Tk = k.shape[1]
    nq, nk = Tq // tq, Tk // tk
    nt_a = qn2.shape[1]
    norm_tiles = (max(1, nt_a // nq), max(1, nt_a // nk), nt_a)
    qi_l, ki_l, last_l = [], [], []
    for qi in range(nq):
        last_chunk = (q_off + qi * tq + tq - 1) // CHUNK
        kmax = min(nk - 1, ((last_chunk + 1) * CHUNK - 1) // tk)
        for ki in range(kmax + 1):
            qi_l.append(qi), ki_l.append(ki), last_l.append(int(ki == kmax))
    tabs = [jnp.asarray(np.asarray(a, np.int32)) for a in (qi_l, ki_l, last_l)]
    tabs += [qn2.reshape(-1), kn2.reshape(-1)]
    grid_spec = pltpu.PrefetchScalarGridSpec(
        num_scalar_prefetch=5, grid=(B, len(qi_l)),
        in_specs=[pl.BlockSpec((1, nqf, tq), lambda b, p, qi, ki, *_: (b, 0, qi[p])),
                  pl.BlockSpec((1, tk, MLA_HEADS * HEAD_PAD), lambda b, p, qi, ki, *_: (b, ki[p], 0)),
                  pl.BlockSpec((1, MLA_HEADS * V_HEAD, tk), lambda b, p, qi, ki, *_: (b, 0, ki[p]))],
        out_specs=pl.BlockSpec((1, tq, MLA_HEADS * V_HEAD), lambda b, p, qi, ki, *_: (b, qi[p], 0)),
        scratch_shapes=[pltpu.VMEM((MLA_HEADS, tq), F32), pltpu.VMEM((MLA_HEADS, tq), F32),
                        pltpu.VMEM((MLA_HEADS * V_HEAD, tq), F32), pltpu.VMEM((2, tk, sw), F32),
                        pltpu.VMEM((2, tk, sw), BF16), pltpu.SMEM((MLA_HEADS,), F32)])
    return pl.pallas_call(
        functools.partial(_attn_kernel, tq, tk, sw, q_off, norm_tiles), grid_spec=grid_spec,
        out_shape=jax.ShapeDtypeStruct((B, Tq, MLA_HEADS * V_HEAD), BF16),
        compiler_params=pltpu.CompilerParams(dimension_semantics=("parallel", "arbitrary"),
                                             vmem_limit_bytes=VMEM_LIMIT),
        name="attn",
    )(*tabs, qt, k, vt)


def _gla_kernel(lc, n_chunks, nch, gq_ref, gk_ref, lg_ref, gv_ref, gg_ref, gon_ref, s0_ref, og_ref, sfin_ref,
                st_ref):
    n_sub = lc // SUB
    n_stack = SUB * n_sub * (n_sub - 1) // 2
    stack_pad = -(-n_stack // LANES) * LANES

    @pl.when(pl.program_id(1) == 0)
    def _():
        st_ref[...] = s0_ref[0]

    lane_k = lax.broadcasted_iota(jnp.int32, (lc, GLA_K), 1) // GLA_DK
    ri = lax.broadcasted_iota(jnp.int32, (lc, lc), 0)
    ci = lax.broadcasted_iota(jnp.int32, (lc, lc), 1)
    tri = (ci <= ri).astype(BF16)
    e_r = lax.broadcasted_iota(jnp.int32, (GLA_K, GLA_V), 0) // GLA_DK
    e_c = lax.broadcasted_iota(jnp.int32, (GLA_K, GLA_V), 1) // GLA_DV
    e_mat = (e_r == e_c).astype(BF16)
    sub_row = lax.broadcasted_iota(jnp.int32, (SUB, GLA_K), 0)
    st_lane = lax.broadcasted_iota(jnp.int32, (GLA_DV, GLA_K), 1) // GLA_DK
    a_row = lax.broadcasted_iota(jnp.int32, (lc, stack_pad), 0) // SUB
    a_col = lax.broadcasted_iota(jnp.int32, (lc, stack_pad), 1)
    off_mask = jnp.zeros((lc, stack_pad), jnp.bool_)
    seg = 0
    for i_sub in range(1, n_sub):
        off_mask = off_mask | ((a_row == i_sub) & (a_col >= seg) & (a_col < seg + SUB * i_sub))
        seg += SUB * i_sub

    def heads_on_rows(a):
        return jnp.concatenate([jnp.where(lane_k == h, a, 0.0) for h in range(GLA_HEADS)], axis=0).astype(BF16)

    def load(c):
        r = pl.ds(pl.multiple_of(c * lc, lc), lc)
        lg = lg_ref[0, r, :]
        hi = lg.astype(BF16)
        r1 = lg - hi.astype(F32)
        mid = r1.astype(BF16)
        lo = (r1 - mid.astype(F32)).astype(BF16)
        b = _dot(tri, hi) + _dot(tri, mid) + _dot(tri, lo)
        return dict(r=r, q=gq_ref[0, r, :], k=gk_ref[0, r, :], v=gv_ref[0, r, :].astype(F32), b=b)

    def local(d):
        q, k, v, b = d["q"], d["k"], d["v"], d["b"]
        v_b = v.astype(BF16)
        b_last = b[lc - 1:lc, :]
        d["decay"] = jnp.exp(b_last)
        d["q4"] = heads_on_rows(q * jnp.exp(b))
        kd = (k * jnp.exp(b_last - b)).astype(BF16)
        full = _tn(v_b, kd)
        upd = jnp.zeros((GLA_DV, GLA_K), F32)
        for h in range(GLA_HEADS):
            upd = upd + jnp.where(st_lane == h, full[h * GLA_DV:(h + 1) * GLA_DV], 0.0)
        d["upd"] = upd
        refs = [jnp.zeros((SUB, GLA_K), F32)]
        kst, vst = [], []
        for i_sub in range(1, n_sub):
            r_i = b[SUB * i_sub - 1:SUB * i_sub, :]
            refs.append(jnp.broadcast_to(r_i, (SUB, GLA_K)))
            kst.append(k[:SUB * i_sub] * jnp.exp(r_i - b[:SUB * i_sub]))
            vst.append(v[:SUB * i_sub])
        if stack_pad > n_stack:
            kst.append(jnp.zeros((stack_pad - n_stack, GLA_K), F32))
            vst.append(jnp.zeros((stack_pad - n_stack, GLA_V), F32))
        kst = jnp.concatenate(kst, axis=0).astype(BF16)
        vst = jnp.concatenate(vst, axis=0).astype(BF16)
        qp = q * jnp.exp(b - jnp.concatenate(refs, axis=0))
        a4 = _nt(heads_on_rows(qp), kst)
        d["o_off"] = [
            _dot(jnp.where(off_mask, a4[h * lc:(h + 1) * lc], 0.0).astype(BF16), vst[:, h * GLA_DV:(h + 1) * GLA_DV])
            for h in range(GLA_HEADS)]

    def same_sub(d):
        q, k, v, b = d["q"], d["k"], d["v"], d["b"]
        terms = []
        for i_sub in range(n_sub):
            rs = slice(SUB * i_sub, SUB * (i_sub + 1))
            q_i, k_i, b_i = q[rs], k[rs], b[rs]
            for j in range(SUB):
                arg = jnp.where(sub_row >= j, b_i - b_i[j:j + 1], NEG)
                terms.append(q_i * (k_i[j:j + 1] * jnp.exp(arg)))
        w = _dot(jnp.concatenate(terms, axis=0).astype(BF16), e_mat)
        o_diag = []
        for i_sub in range(n_sub):
            acc = None
            for j in range(SUB):
                row = SUB * i_sub + j
                t = w[row * SUB:(row + 1) * SUB] * v[row:row + 1]
                acc = t if acc is None else acc + t
            o_diag.append(acc)
        d["o_diag"] = jnp.concatenate(o_diag, axis=0)

    def carried(d):
        st = st_ref[...]
        d["o_heads"] = _nt(d["q4"], st.astype(BF16))
        st_ref[...] = d["decay"] * st + d["upd"]

    def finish(d):
        o = jnp.concatenate(
            [d["o_heads"][h * lc:(h + 1) * lc] + d["o_off"][h] for h in range(GLA_HEADS)], axis=1) + d["o_diag"]
        gon = gon_ref[...]
        normed = jnp.concatenate(
            [_rms(o[:, h * GLA_DV:(h + 1) * GLA_DV], gon) for h in range(GLA_HEADS)], axis=1)
        gg = gg_ref[0, d["r"], :]
        og_ref[0, d["r"], :] = (normed * (gg * jax.nn.sigmoid(gg))).astype(BF16)

    def body(it, carry):
        ds = [load(it * nch + u) for u in range(nch)]
        for stage in (local, same_sub, carried, finish):
            for d in ds:
                stage(d)
        return carry

    lax.fori_loop(0, n_chunks // nch, body, 0)

    @pl.when(pl.program_id(1) == pl.num_programs(1) - 1)
    def _():
        sfin_ref[0] = st_ref[...]


def _gla(gq, gk, lg, gv, gg, g_on, s0t, lc, tg):
    B, T, _ = gq.shape
    ng = T // tg
    tok = lambda w: pl.BlockSpec((1, tg, w), lambda b, g: (b, g, 0))
    st_spec = pl.BlockSpec((1, GLA_DV, GLA_K), lambda b, g: (b, 0, 0))
    S = jax.ShapeDtypeStruct
    return pl.pallas_call(
        functools.partial(_gla_kernel, lc, tg // lc, _pick(tg // lc, (4, 2))), grid=(B, ng),
        in_specs=[tok(GLA_K), tok(GLA_K), tok(GLA_K), tok(GLA_V), tok(GLA_V),
                  pl.BlockSpec((1, GLA_DV), lambda b, g: (0, 0)), st_spec],
        out_specs=[tok(GLA_V), st_spec],
        out_shape=[S((B, T, GLA_V), BF16), S((B, GLA_DV, GLA_K), F32)],
        scratch_shapes=[pltpu.VMEM((GLA_DV, GLA_K), F32)],
        compiler_params=pltpu.CompilerParams(dimension_semantics=("parallel", "arbitrary"),
                                             vmem_limit_bytes=VMEM_LIMIT),
        name="gla",
    )(gq, gk, lg, gv, gg, g_on, s0t)


def _ffn_kernel(final, x_ref, om_ref, og_ref, wo_ref, gf_ref, wg_ref, wu_ref, wd_ref, gfin_ref, y_ref):
    cat = jnp.concatenate([om_ref[0], og_ref[0]], axis=1)
    x1 = x_ref[0] + _dot(cat, wo_ref[...])
    hn = _rms(x1, gf_ref[...]).astype(BF16)
    a = _dot(hn, wg_ref[...])
    u = _dot(hn, wu_ref[...])
    act = (a * jax.nn.sigmoid(a) * u).astype(BF16)
    x2 = x1 + _dot(act, wd_ref[...])
    y_ref[0] = _rms(x2, gfin_ref[...]) if final else x2


def _ffn(x, o_mla, o_g, lw, l, w_gu, w_down, g_final, tm, final):
    B, T, D = x.shape
    tok = lambda w: pl.BlockSpec((1, tm, w), lambda b, t: (b, t, 0))
    const = lambda a: pl.BlockSpec(a.shape, lambda b, t: (0,) * a.ndim, pipeline_mode=pl.Buffered(1))
    d_ff = w_down.shape[1]
    half = lambda j: pl.BlockSpec((None, D, d_ff), lambda b, t: (l, 0, j), pipeline_mode=pl.Buffered(1))
    down = pl.BlockSpec((None, d_ff, D), lambda b, t: (l, 0, 0), pipeline_mode=pl.Buffered(1))
    weights = (lw["w_o"], lw["g_ffn"], w_gu, w_gu, w_down, g_final)
    w_specs = [const(lw["w_o"]), const(lw["g_ffn"]), half(0), half(1), down, const(g_final)]
    return pl.pallas_call(
        functools.partial(_ffn_kernel, final), grid=(B, T // tm),
        in_specs=[tok(D), tok(o_mla.shape[-1]), tok(o_g.shape[-1])] + w_specs,
        out_specs=tok(D), out_shape=jax.ShapeDtypeStruct((B, T, D), F32),
        compiler_params=pltpu.CompilerParams(dimension_semantics=("parallel", "parallel"),
                                             vmem_limit_bytes=VMEM_LIMIT),
        name="ffn_final" if final else "ffn",
    )(x, o_mla, o_g, *weights)


def _layer_weights(l, g_attn, w_in, g_qn, w_uq, g_kvn, w_ukv, w_a2, b_a2, g_gla_on, w_o, g_ffn):
    d_model = w_in.shape[1]
    wi = w_in[l]
    c = np.cumsum([0, Q_LORA, KV_LORA, QK_ROPE, GLA_K, GLA_K, GLA_V, GLA_V, GATE_LR])
    sm = jnp.concatenate([wi[:, c[2]:c[3]], wi[:, c[7]:c[8]],
                          jnp.zeros((d_model, LANES - QK_ROPE - GATE_LR), F32)], axis=1)
    w_in_r = jnp.concatenate([wi[:, c[0]:c[2]], wi[:, c[3]:c[4]] * (GLA_DK ** -0.5), wi[:, c[4]:c[7]], sm], axis=1)
    wq = w_uq[l].reshape(Q_LORA, MLA_HEADS, QK_NOPE + QK_ROPE)
    w_qn_t = wq[:, :, :QK_NOPE].reshape(Q_LORA, -1).T
    w_qr_t = jnp.concatenate([wq[:, :, QK_NOPE:QK_NOPE + ROPE_HALF].reshape(Q_LORA, -1),
                              wq[:, :, QK_NOPE + ROPE_HALF:].reshape(Q_LORA, -1)], axis=1).T
    wkv = w_ukv[l].reshape(KV_LORA, MLA_HEADS, QK_NOPE + V_HEAD)
    w_k = jnp.pad(wkv[:, :, :QK_NOPE], ((0, 0), (0, 0), (0, HEAD_PAD - QK_NOPE))).reshape(KV_LORA, -1)
    w_v_t = wkv[:, :, QK_NOPE:].reshape(KV_LORA, -1).T
    w_uk_t = wkv[:, :, :QK_NOPE].transpose(1, 2, 0)
    w_uv_pad = jnp.zeros((MLA_HEADS, KV_LORA, MLA_HEADS * V_HEAD), F32)
    for hd in range(MLA_HEADS):
        w_uv_pad = w_uv_pad.at[hd, :, hd * V_HEAD:(hd + 1) * V_HEAD].set(wkv[:, hd, QK_NOPE:])
    w_a2_p = jnp.zeros((LANES, GLA_K), F32).at[QK_ROPE:QK_ROPE + GATE_LR].set(w_a2[l])
    row = lambda v: v.reshape(1, -1)
    n_nope, n_rope = MLA_HEADS * QK_NOPE, MLA_HEADS * ROPE_HALF
    e_q = np.zeros((BF16_ROWS, n_nope + 2 * n_rope), np.float32)
    e_k = np.zeros((MLA_HEADS * HEAD_PAD, LANES), np.float32)
    for h in range(MLA_HEADS):
        e_q[h, h * QK_NOPE:(h + 1) * QK_NOPE] = 1.0
        e_q[h, n_nope + h * ROPE_HALF:n_nope + (h + 1) * ROPE_HALF] = 1.0
        e_q[h, n_nope + n_rope + h * ROPE_HALF:n_nope + n_rope + (h + 1) * ROPE_HALF] = 1.0
        e_k[h * HEAD_PAD:(h + 1) * HEAD_PAD, h] = 1.0
    return {
        "e_q": jnp.asarray(e_q, BF16), "e_k": jnp.asarray(e_k, BF16),
        "g_attn": row(g_attn[l]), "w_in": w_in_r.astype(BF16),
        "g_qn": row(g_qn[l] * (ATTN_SCALE * LOG2E)), "w_qnT": w_qn_t.astype(BF16), "w_qrT": w_qr_t.astype(BF16),
        "g_kvn": row(g_kvn[l]), "w_k": w_k.astype(BF16), "w_vT": w_v_t.astype(BF16),
        "w_ukT": w_uk_t.astype(BF16), "w_uv_pad": w_uv_pad.astype(BF16),
        "w_a2": w_a2_p.astype(BF16), "b_a2": row(b_a2[l]),
        "g_on": row(g_gla_on[l]), "w_o": w_o[l].astype(BF16), "g_ffn": row(g_ffn[l]),
    }


def _rope_tables(pos_off, pos_base):
    inv = ROPE_THETA ** (-jnp.arange(ROPE_HALF, dtype=F32) / ROPE_HALF)

    def tables(pos):
        ang = pos.astype(F32)[:, None] * inv[None, :]
        cos, sin = jnp.cos(ang), jnp.sin(ang)
        zeros = jnp.zeros((pos.shape[0], LANES - QK_ROPE), F32)
        rows = [jnp.concatenate([a, a, zeros], axis=1) for a in (cos, sin)]
        cols = [jnp.tile(a, (1, MLA_HEADS)).T for a in (cos, sin)]
        return rows, cols

    (cro, sro), (cto, sto) = tables(pos_off)
    (crb, srb), (ctb, stb) = tables(pos_base)
    nt = pos_base.shape[0]
    base_r = lambda a: a.reshape(nt, 1, LANES)
    base_t = lambda a: jnp.broadcast_to(a.T[:, :, None], (nt, LANES, LANES))
    return cro, sro, base_r(crb), base_r(srb), cto, sto, base_t(ctb), base_t(stb)


def _pick(n, pref):
    for t in pref:
        if n % t == 0:
            return t
    return n


def kernel(x_prompt, x_sample, cache_ckv, cache_krope, state_gla, g_attn, w_in, g_qn, w_uq, g_kvn, w_ukv, w_a2, b_a2,
           g_gla_on, w_o, g_ffn, w_gu, w_down, g_final):
    depth = w_in.shape[0]
    B, T, D = x_prompt.shape
    nb, ts, _ = x_sample.shape
    past = cache_ckv.shape[2]
    assert T % CHUNK == 0 and ts % BF16_ROWS == 0 and ts <= CHUNK

    tm_p = _pick(T, (512, 256, 128))
    tabs_p = _rope_tables(jnp.arange(tm_p, dtype=jnp.int32), jnp.arange(T // tm_p, dtype=jnp.int32) * tm_p)
    tabs_s = _rope_tables(jnp.tile(past + jnp.arange(ts, dtype=jnp.int32), nb), jnp.zeros((1,), jnp.int32))
    w_gu_b, w_down_b = w_gu.astype(BF16), w_down.astype(BF16)
    tk = _pick(T, (ATTN_TK, 256, 128, 64))
    tq = ATTN_TQ if (T % ATTN_TQ == 0 and ATTN_TQ % tk == 0) else tk
    tg = _pick(T, (512, 256, 128, 64))
    g_fin = g_final.reshape(1, -1)

    xp = x_prompt
    xs = x_sample.reshape(1, nb * ts, D)
    outs = {k: [] for k in ("ckv_p", "kr_p", "st_p", "ckv_s", "kr_s", "st_s")}
    for l in range(depth):
        lw = _layer_weights(l, g_attn, w_in, g_qn, w_uq, g_kvn, w_ukv, w_a2, b_a2, g_gla_on, w_o, g_ffn)
        final = l == depth - 1
        ckv, kr, qt, k, vt, qn2, kn2, gq, gk, gv, gg, lg = _proj(xp, tabs_p, lw, tm_p, True)
        o_mla = _attn(qt, k, vt, qn2[:, :, :MLA_HEADS, 0], kn2[:, :, 0, :MLA_HEADS], tq, tk, tk, 0)
        s0 = jnp.zeros((B, GLA_DV, GLA_K), F32)
        o_g, st_fin = _gla(gq, gk, lg, gv, gg, lw["g_on"], s0, CHUNK, tg)
        xp = _ffn(xp, o_mla, o_g, lw, l, w_gu_b, w_down_b, g_fin, tm_p, final)
        outs["ckv_p"].append(ckv), outs["kr_p"].append(kr)
        outs["st_p"].append(st_fin.reshape(B, GLA_DV, GLA_HEADS, GLA_DK).transpose(0, 2, 3, 1))
        ckv, kr, qt, gq, gk, gv, gg, lg = _proj(xs, tabs_s, lw, nb * ts, False)
        ckv, kr = ckv.reshape(nb, ts, -1), kr.reshape(nb, ts, -1)
        n_nope, n_rope = MLA_HEADS * QK_NOPE, MLA_HEADS * ROPE_HALF
        per_head = lambda a: a.reshape(MLA_HEADS, -1, nb * ts)
        qt = jnp.concatenate([per_head(qt[0, :n_nope]), per_head(qt[0, n_nope:n_nope + n_rope]),
                              per_head(qt[0, n_nope + n_rope:]),
                              jnp.zeros((MLA_HEADS, HEAD_PAD - QK_NOPE - QK_ROPE, nb * ts), BF16)], axis=1)
        qt = qt.reshape(MLA_HEADS, HEAD_PAD, nb, ts).transpose(2, 0, 1, 3)
        o_mla = _cache_attn(l, cache_ckv, cache_krope, ckv, kr, qt, lw).reshape(1, nb * ts, -1)
        s0 = state_gla[l].astype(F32).transpose(0, 3, 1, 2).reshape(nb, GLA_DV, GLA_K)
        per_b = lambda a: a.reshape(nb, ts, -1)
        o_g, st_fin = _gla(per_b(gq), per_b(gk), per_b(lg), per_b(gv), per_b(gg), lw["g_on"], s0, ts, ts)
        xs = _ffn(xs, o_mla, o_g.reshape(1, nb * ts, -1), lw, l, w_gu_b, w_down_b, g_fin, nb * ts, final)
        outs["ckv_s"].append(ckv), outs["kr_s"].append(kr)
        outs["st_s"].append(st_fin.reshape(nb, GLA_DV, GLA_HEADS, GLA_DK).transpose(0, 2, 3, 1))

    stack = lambda key: jnp.stack(outs[key])
    return (xp, xs.reshape(nb, ts, D), stack("ckv_p"), stack("kr_p"), stack("st_p"),
            stack("ckv_s"), stack("kr_s"), stack("st_s"))
```
